```python
import functools
import jax, jax.numpy as jnp
from jax import lax
import numpy as np

D_MODEL = 2048
BATCH = 2
SEQ = 4096
DEPTH = 2
DEC_BATCH = 128
DEC_SEQ = 4
PAST_LEN = 8192
PAGE_SIZE = 128

N_META = 16
HEAD_DIM = 64
ATTN_WIDTH = D_MODEL // 2
N_HEADS = ATTN_WIDTH // HEAD_DIM
N_KV_HEADS = 2
GROUP = N_HEADS // N_KV_HEADS
KV_WIDTH = N_KV_HEADS * HEAD_DIM
WINDOW = 128
BLOCK = WINDOW
CONV_CH = D_MODEL - ATTN_WIDTH
CONV_WIDTH = 31
MIX_WIDTH = ATTN_WIDTH + CONV_CH
IN_COLS = ATTN_WIDTH + 2 * KV_WIDTH + 2 * CONV_CH
D_FF = 5504
FFN_CONV_WIDTH = 3
LN_EPS = 1e-5
ALPHA = (2 * DEPTH) ** 0.25
BETA = (8 * DEPTH) ** -0.25

kernel_name = 'hymba_conformer_swa_sink_convffn_step'


def layer_norm(x, g, b):
    xf = x.astype(jnp.float32)
    mu = xf.mean(-1, keepdims=True)
    var = jnp.square(xf - mu).mean(-1, keepdims=True)
    return ((xf - mu) * lax.rsqrt(var + LN_EPS) * g + b).astype(x.dtype)


def causal_dwconv(hist, x, w, b):
    width = w.shape[0]
    xf = jnp.concatenate([hist.astype(x.dtype), x], axis=1)
    y = lax.conv_general_dilated(xf, w.astype(x.dtype)[:, None, :], window_strides=(1,), padding='VALID',
                                 dimension_numbers=('NWC', 'WIO', 'NWC'), feature_group_count=x.shape[-1])
    return y + b, xf[:, xf.shape[1] - (width - 1):]


def attend_with_sinks(q, k, v, mask, sinks):
    s = jnp.einsum('...qkgd,...skd->...kgqs', q, k).astype(jnp.float32) * (HEAD_DIM ** -0.5)
    s = jnp.where(mask, s, -jnp.inf)
    sink = sinks.astype(jnp.float32).reshape(N_KV_HEADS, GROUP, 1, 1)
    m = jnp.maximum(s.max(-1, keepdims=True), sink)
    p = jnp.exp(s - m)
    denom = p.sum(-1, keepdims=True) + jnp.exp(sink - m)
    return jnp.einsum('...kgqs,...skd->...qkgd', (p / denom).astype(v.dtype), v)


def swa_prompt(q, k, v, sinks):
    B, L = q.shape[:2]
    pad = (-L) % BLOCK
    nb = (L + pad) // BLOCK
    qb = jnp.pad(q, ((0, 0), (pad, 0), (0, 0), (0, 0), (0, 0))).reshape(B, nb, BLOCK, N_KV_HEADS, GROUP, HEAD_DIM)
    kb = jnp.pad(k, ((0, 0), (pad, 0), (0, 0), (0, 0))).reshape(B, nb, BLOCK, N_KV_HEADS, HEAD_DIM)
    vb = jnp.pad(v, ((0, 0), (pad, 0), (0, 0), (0, 0))).reshape(B, nb, BLOCK, N_KV_HEADS, HEAD_DIM)
    blk_pad = ((0, 0), (1, 0), (0, 0), (0, 0), (0, 0))
    k_band = jnp.concatenate([jnp.pad(kb, blk_pad)[:, :-1], kb], axis=2)
    v_band = jnp.concatenate([jnp.pad(vb, blk_pad)[:, :-1], vb], axis=2)
    blocks = jnp.arange(nb)[:, None]
    q_pos = blocks * BLOCK + jnp.arange(BLOCK)[None, :] - pad
    k_pos = (blocks - 1) * BLOCK + jnp.arange(2 * BLOCK)[None, :] - pad
    diff = q_pos[:, :, None] - k_pos[:, None, :]
    mask = (k_pos[:, None, :] >= 0) & (diff >= 0) & (diff <= WINDOW)
    o = attend_with_sinks(qb, k_band, v_band, mask[None, :, None, None], sinks)
    o = o.reshape(B, nb * BLOCK, ATTN_WIDTH)[:, pad:]
    return o, k[:, L - WINDOW:], v[:, L - WINDOW:]


def swa_sample(q, k, v, hist_k, hist_v, sinks):
    Bd, T = q.shape[:2]
    k_all = jnp.concatenate([hist_k.astype(k.dtype), k], axis=1)
    v_all = jnp.concatenate([hist_v.astype(v.dtype), v], axis=1)
    q_pos = PAST_LEN + jnp.arange(T)
    k_pos = PAST_LEN - WINDOW + jnp.arange(WINDOW + T)
    diff = q_pos[:, None] - k_pos[None, :]
    mask = (k_pos[None, :] >= 0) & (diff >= 0) & (diff <= WINDOW)
    o = attend_with_sinks(q, k_all, v_all, mask[None, None, None], sinks)
    n = k_all.shape[1]
    return o.reshape(Bd, T, ATTN_WIDTH), k_all[:, n - WINDOW:], v_all[:, n - WINDOW:]


def trunk_layer(x, attend_fn, conv_hist, ffn_hist, w_in, b_in, conv_w, conv_b, conv_ln_g, conv_ln_b,
                w_out, b_out, ln1_g, ln1_b, ffn_w_up, ffn_conv_w, ffn_conv_b, ffn_w_down, ln2_g, ln2_b):
    B, T, _ = x.shape
    proj = jnp.einsum('btd,dc->btc', x, w_in) + b_in
    q, k, v, glu = jnp.split(proj, [ATTN_WIDTH, ATTN_WIDTH + KV_WIDTH, ATTN_WIDTH + 2 * KV_WIDTH], axis=-1)
    q = q.reshape(B, T, N_KV_HEADS, GROUP, HEAD_DIM)
    k = k.reshape(B, T, N_KV_HEADS, HEAD_DIM)
    v = v.reshape(B, T, N_KV_HEADS, HEAD_DIM)
    attn, new_k, new_v = attend_fn(q, k, v)
    a, g = jnp.split(glu, 2, axis=-1)
    c, new_conv = causal_dwconv(conv_hist, a * jax.nn.sigmoid(g), conv_w, conv_b)
    c = jax.nn.silu(layer_norm(c, conv_ln_g, conv_ln_b))
    mix = jnp.einsum('btm,md->btd', jnp.concatenate([attn, c], axis=-1), w_out) + b_out
    x = layer_norm(ALPHA * x + mix, ln1_g, ln1_b)
    h = jnp.einsum('btd,df->btf', x, ffn_w_up)
    h, new_ffn = causal_dwconv(ffn_hist, h, ffn_conv_w, ffn_conv_b)
    hu, hg = jnp.split(h, 2, axis=-1)
    f = jnp.einsum('btf,fd->btd', jax.nn.silu(hg) * hu, ffn_w_down)
    x = layer_norm(ALPHA * x + f, ln2_g, ln2_b)
    return x, new_k, new_v, new_conv, new_ffn


def setup_inputs(seed: int = 0) -> dict:
    key = jax.random.key(seed)
    ks = jax.random.split(key, 24)

    def nrm(k, shape, scale):
        return jax.random.normal(k, shape, jnp.float32) * scale

    return {
        'x_prompt': nrm(ks[0], (BATCH, SEQ, D_MODEL), 1.0),
        'x_sample': nrm(ks[1], (DEC_BATCH, DEC_SEQ, D_MODEL), 1.0),
        'state_attn_k': nrm(ks[2], (DEPTH, DEC_BATCH, WINDOW, N_KV_HEADS, HEAD_DIM), 1.0),
        'state_attn_v': nrm(ks[3], (DEPTH, DEC_BATCH, WINDOW, N_KV_HEADS, HEAD_DIM), 1.0),
        'state_conv': nrm(ks[4], (DEPTH, DEC_BATCH, CONV_WIDTH - 1, CONV_CH), 0.5),
        'state_ffn_conv': nrm(ks[5], (DEPTH, DEC_BATCH, FFN_CONV_WIDTH - 1, 2 * D_FF), 1.0),
        'meta_tokens': nrm(ks[6], (N_META, D_MODEL), 1.0),
        'w_in': nrm(ks[7], (DEPTH, D_MODEL, IN_COLS), D_MODEL ** -0.5),
        'b_in': nrm(ks[8], (DEPTH, IN_COLS), 0.02),
        'attn_sinks': nrm(ks[9], (DEPTH, N_HEADS), 1.0),
        'conv_w': nrm(ks[10], (DEPTH, CONV_WIDTH, CONV_CH), CONV_WIDTH ** -0.5),
        'conv_b': nrm(ks[11], (DEPTH, CONV_CH), 0.02),
        'conv_ln_g': 1.0 + nrm(ks[12], (DEPTH, CONV_CH), 0.02),
        'conv_ln_b': nrm(ks[13], (DEPTH, CONV_CH), 0.02),
        'w_out': nrm(ks[14], (DEPTH, MIX_WIDTH, D_MODEL), MIX_WIDTH ** -0.5 * BETA),
        'b_out': nrm(ks[15], (DEPTH, D_MODEL), 0.02),
        'ln1_g': 1.0 + nrm(ks[16], (DEPTH, D_MODEL), 0.02),
        'ln1_b': nrm(ks[17], (DEPTH, D_MODEL), 0.02),
        'ffn_w_up': nrm(ks[18], (DEPTH, D_MODEL, 2 * D_FF), D_MODEL ** -0.5),
        'ffn_conv_w': nrm(ks[19], (DEPTH, FFN_CONV_WIDTH, 2 * D_FF), FFN_CONV_WIDTH ** -0.5),
        'ffn_conv_b': nrm(ks[20], (DEPTH, 2 * D_FF), 0.02),
        'ffn_w_down': nrm(ks[21], (DEPTH, D_FF, D_MODEL), D_FF ** -0.5 * BETA),
        'ln2_g': 1.0 + nrm(ks[22], (DEPTH, D_MODEL), 0.02),
        'ln2_b': nrm(ks[23], (DEPTH, D_MODEL), 0.02),
    }


def reference(x_prompt, x_sample, state_attn_k, state_attn_v, state_conv, state_ffn_conv, meta_tokens,
              w_in, b_in, attn_sinks, conv_w, conv_b, conv_ln_g, conv_ln_b, w_out, b_out, ln1_g, ln1_b,
              ffn_w_up, ffn_conv_w, ffn_conv_b, ffn_w_down, ln2_g, ln2_b):
    B = x_prompt.shape[0]
    meta = jnp.broadcast_to(meta_tokens[None].astype(x_prompt.dtype), (B, N_META, D_MODEL))
    xp = jnp.concatenate([meta, x_prompt], axis=1)
    xs = x_sample
    pk, pv, pc, pf, sk, sv, sc, sf = [], [], [], [], [], [], [], []
    for l in range(DEPTH):
        params = (w_in[l], b_in[l], conv_w[l], conv_b[l], conv_ln_g[l], conv_ln_b[l], w_out[l], b_out[l],
                  ln1_g[l], ln1_b[l], ffn_w_up[l], ffn_conv_w[l], ffn_conv_b[l], ffn_w_down[l], ln2_g[l], ln2_b[l])
        xp, k_new, v_new, c_new, f_new = trunk_layer(
            xp, functools.partial(swa_prompt, sinks=attn_sinks[l]),
            jnp.zeros((B, CONV_WIDTH - 1, CONV_CH), xp.dtype),
            jnp.zeros((B, FFN_CONV_WIDTH - 1, 2 * D_FF), xp.dtype), *params)
        pk.append(k_new); pv.append(v_new); pc.append(c_new); pf.append(f_new)
        xs, k_new, v_new, c_new, f_new = trunk_layer(
            xs, functools.partial(swa_sample, hist_k=state_attn_k[l], hist_v=state_attn_v[l], sinks=attn_sinks[l]),
            state_conv[l], state_ffn_conv[l], *params)
        sk.append(k_new); sv.append(v_new); sc.append(c_new); sf.append(f_new)
    y_prompt = xp[:, N_META:]
    return (y_prompt, xs,
            jnp.stack(pk), jnp.stack(pv), jnp.stack(pc), jnp.stack(pf),
            jnp.stack(sk), jnp.stack(sv), jnp.stack(sc), jnp.stack(sf))
```

```python
import functools

import jax
import jax.numpy as jnp
from jax.experimental import pallas as pl
from jax.experimental.pallas import tpu as pltpu

F32 = jnp.float32
BF16 = jnp.bfloat16

N_META = 16
HEAD_DIM = 64
N_KV_HEADS = 2
WINDOW = 128
LN_EPS = 1e-5
LANES = 128
SUBLANES = 8
FF_CHUNK = 512
VMEM_LIMIT = 56 * 1024 * 1024


def _round_up(x, m):
    return (x + m - 1) // m * m


def _layer_norm(y, g, b):
    mu = jnp.mean(y, axis=-1, keepdims=True)
    d = y - mu
    var = jnp.mean(d * d, axis=-1, keepdims=True)
    return d * jax.lax.rsqrt(var + LN_EPS) * g + b


def _params(*sem):
    return pltpu.CompilerParams(dimension_semantics=sem, vmem_limit_bytes=VMEM_LIMIT)


def _in_proj_kernel(x_ref, w_ref, b_ref, q_ref, kv_ref, u_ref, *, attn_w, conv_ch, kv_w, q_scale):
    xb = x_ref[...].astype(BF16)

    def proj(c0, n):
        return jnp.dot(xb, w_ref[:, c0:c0 + n], preferred_element_type=F32) + b_ref[:, c0:c0 + n]

    q_ref[...] = (proj(0, attn_w) * q_scale).astype(BF16)
    a = proj(attn_w, conv_ch)
    g = proj(attn_w + conv_ch, conv_ch)
    u_ref[...] = a * jax.nn.sigmoid(g)
    kv_ref[...] = proj(attn_w + 2 * conv_ch, kv_w)


def _in_proj(x, w, b, *, tm, attn_w, conv_ch, kv_w):
    rows, d = x.shape
    n = w.shape[1]
    kern = functools.partial(_in_proj_kernel, attn_w=attn_w, conv_ch=conv_ch, kv_w=kv_w,
                             q_scale=HEAD_DIM ** -0.5)
    return pl.pallas_call(
        kern,
        grid=(rows // tm,),
        in_specs=[pl.BlockSpec((tm, d), lambda i: (i, 0)),
                  pl.BlockSpec((d, n), lambda i: (0, 0), pipeline_mode=pl.Buffered(1)),
                  pl.BlockSpec((1, n), lambda i: (0, 0))],
        out_specs=[pl.BlockSpec((tm, attn_w), lambda i: (i, 0)),
                   pl.BlockSpec((tm, kv_w), lambda i: (i, 0)),
                   pl.BlockSpec((tm, conv_ch), lambda i: (i, 0))],
        out_shape=[jax.ShapeDtypeStruct((rows, attn_w), BF16),
                   jax.ShapeDtypeStruct((rows, kv_w), F32),
                   jax.ShapeDtypeStruct((rows, conv_ch), F32)],
        compiler_params=_params("arbitrary"),
        name="in_proj",
    )(x, w, b)


def _conv_kernel(hist_ref, u_ref, w_ref, cb_ref, g_ref, b_ref, c_ref, ext_ref, *, tm, prev, stride, taps, rc):
    t = pl.program_id(1)

    @pl.when(t == 0)
    def _():
        ext_ref[0:prev, :] = hist_ref[0]

    @pl.when(t > 0)
    def _():
        ext_ref[0:prev, :] = ext_ref[tm:tm + prev, :]

    ext_ref[prev:prev + tm, :] = u_ref[...]
    base = prev - stride * (taps - 1)
    for r0 in range(0, tm, rc):
        acc = w_ref[0:1, :] * ext_ref[base + r0:base + r0 + rc, :]
        for j in range(1, taps):
            off = base + stride * j + r0
            acc = acc + w_ref[j:j + 1, :] * ext_ref[off:off + rc, :]
        y = _layer_norm(acc + cb_ref[...], g_ref[...], b_ref[...])
        c_ref[r0:r0 + rc, :] = (y * jax.nn.sigmoid(y)).astype(BF16)


def _conv_module(hist, u, w, cb, g, b, *, tm, tiles_per_seg, stride):
    rows, ch = u.shape
    n_seg, prev, _ = hist.shape
    taps = w.shape[0]
    kern = functools.partial(_conv_kernel, tm=tm, prev=prev, stride=stride, taps=taps, rc=32)
    return pl.pallas_call(
        kern,
        grid=(n_seg, tiles_per_seg),
        in_specs=[pl.BlockSpec((1, prev, ch), lambda s, t: (s, 0, 0)),
                  pl.BlockSpec((tm, ch), lambda s, t: (s * tiles_per_seg + t, 0)),
                  pl.BlockSpec((taps, ch), lambda s, t: (0, 0)),
                  pl.BlockSpec((1, ch), lambda s, t: (0, 0)),
                  pl.BlockSpec((1, ch), lambda s, t: (0, 0)),
                  pl.BlockSpec((1, ch), lambda s, t: (0, 0))],
        out_specs=pl.BlockSpec((tm, ch), lambda s, t: (s * tiles_per_seg + t, 0)),
        out_shape=jax.ShapeDtypeStruct((rows, ch), BF16),
        scratch_shapes=[pltpu.VMEM((prev + tm, ch), F32)],
        compiler_params=_params("arbitrary", "arbitrary"),
        name="conv_module",
    )(hist, u, w, cb, g, b)


def _conv_sample_kernel(hist_ref, u_ref, w_ref, cb_ref, g_ref, b_ref, c_ref):
    n_hist = hist_ref.shape[0]
    for t in range(u_ref.shape[0]):
        acc = None
        for j in range(w_ref.shape[0]):
            s = t + j
            src = hist_ref[s] if s < n_hist else u_ref[s - n_hist]
            term = w_ref[j:j + 1, :] * src
            acc = term if acc is None else acc + term
        y = _layer_norm(acc + cb_ref[...], g_ref[...], b_ref[...])
        c_ref[t] = (y * jax.nn.sigmoid(y)).astype(BF16)


def _conv_module_sample(hist, u, w, cb, g, b, *, bb):
    t_steps, n_seq, ch = u.shape
    n_hist = hist.shape[0]
    taps = w.shape[0]
    assert n_hist == taps - 1
    return pl.pallas_call(
        _conv_sample_kernel,
        grid=(n_seq // bb,),
        in_specs=[pl.BlockSpec((n_hist, bb, ch), lambda i: (0, i, 0)),
                  pl.BlockSpec((t_steps, bb, ch), lambda i: (0, i, 0)),
                  pl.BlockSpec((taps, ch), lambda i: (0, 0)),
                  pl.BlockSpec((1, ch), lambda i: (0, 0)),
                  pl.BlockSpec((1, ch), lambda i: (0, 0)),
                  pl.BlockSpec((1, ch), lambda i: (0, 0))],
        out_specs=pl.BlockSpec((t_steps, bb, ch), lambda i: (0, i, 0)),
        out_shape=jax.ShapeDtypeStruct((t_steps, n_seq, ch), BF16),
        compiler_params=_params("arbitrary"),
        name="conv_module_sample",
    )(hist, u, w, cb, g, b)


def _attn_prompt_kernel(sink_ref, q_ref, kvp_ref, kvc_ref, o_ref, *, n_heads, group):
    blk = q_ref.shape[0]
    i = pl.program_id(1)
    kv = jnp.concatenate([kvp_ref[...], kvc_ref[...]], axis=0)
    kband = kv[:, 0:LANES]
    vband = kv[:, LANES:2 * LANES]
    kroll = pltpu.roll(kband, HEAD_DIM, 1)
    vroll = pltpu.roll(vband, HEAD_DIM, 1)
    lane = jax.lax.broadcasted_iota(jnp.int32, (2 * blk, LANES), 1)
    lo = lane < HEAD_DIM

    def lo_hi(x_lo, x_hi):
        return jnp.concatenate([jnp.where(lo, x_lo, 0.0), jnp.where(lo, 0.0, x_hi)], axis=0).astype(BF16)

    kab = [lo_hi(kband, kroll), lo_hi(kroll, kband)]
    vab = [lo_hi(vband, vroll), lo_hi(vroll, vband)]

    r = jax.lax.broadcasted_iota(jnp.int32, (blk, 2 * blk), 0)
    c = jax.lax.broadcasted_iota(jnp.int32, (blk, 2 * blk), 1)
    mask = (c >= r) & (c <= r + WINDOW) & ((c >= blk) | (i > 0))
    out_lo = jax.lax.broadcasted_iota(jnp.int32, (blk, LANES), 1) < HEAD_DIM

    for pair in range(n_heads // 2):
        kvh = (2 * pair) // group
        q2 = q_ref[:, pair * LANES:(pair + 1) * LANES]
        s2 = jax.lax.dot_general(q2, kab[kvh], (((1,), (1,)), ((), ())), preferred_element_type=F32)
        ps, invs = [], []
        for half in range(2):
            sink = sink_ref[2 * pair + half]
            s = jnp.where(mask, s2[:, half * 2 * blk:(half + 1) * 2 * blk], -jnp.inf)
            m = jnp.maximum(jnp.max(s, axis=-1, keepdims=True), sink)
            p = jnp.exp(s - m)
            denom = jnp.sum(p, axis=-1, keepdims=True) + jnp.exp(sink - m)
            ps.append(p.astype(BF16))
            invs.append(1.0 / denom)
        o2 = jnp.dot(jnp.concatenate(ps, axis=1), vab[kvh], preferred_element_type=F32)
        o2 = o2 * jnp.where(out_lo, invs[0], invs[1])
        o_ref[:, pair * LANES:(pair + 1) * LANES] = o2.astype(BF16)


def _attn_prompt(sinks, q, kv, *, n_seg, blocks_per_seg, blk):
    rows, attn_w = q.shape
    kv_w = kv.shape[1]
    n_heads = attn_w // HEAD_DIM
    kern = functools.partial(_attn_prompt_kernel, n_heads=n_heads, group=n_heads // N_KV_HEADS)
    return pl.pallas_call(
        kern,
        grid=(n_seg, blocks_per_seg),
        in_specs=[pl.BlockSpec(memory_space=pltpu.SMEM),
                  pl.BlockSpec((blk, attn_w), lambda s, i: (s * blocks_per_seg + i, 0)),
                  pl.BlockSpec((blk, kv_w), lambda s, i: (jnp.maximum(s * blocks_per_seg + i - 1, 0), 0)),
                  pl.BlockSpec((blk, kv_w), lambda s, i: (s * blocks_per_seg + i, 0))],
        out_specs=pl.BlockSpec((blk, attn_w), lambda s, i: (s * blocks_per_seg + i, 0)),
        out_shape=jax.ShapeDtypeStruct((rows, attn_w), BF16),
        compiler_params=_params("arbitrary", "arbitrary"),
        name="attn_prompt",
    )(sinks, q, kv, kv)


def _attn_sample_kernel(sink_ref, q_ref, hk_ref, hv_ref, nkv_ref, o_ref, *, t_steps, group):
    bb, nq, _ = q_ref.shape
    npad = nkv_ref.shape[1]
    q = q_ref[...]
    hk = hk_ref[...].astype(BF16)
    hv = hv_ref[...].astype(BF16)
    nk = nkv_ref[:, :, 0:LANES].astype(BF16)
    nv = nkv_ref[:, :, LANES:2 * LANES].astype(BF16)
    s_h = jnp.einsum('bqd,bkd->bqk', q, hk, preferred_element_type=F32)
    s_n = jnp.einsum('bqd,bkd->bqk', q, nk, preferred_element_type=F32)
    row = jax.lax.broadcasted_iota(jnp.int32, (1, nq, 1), 1)
    t = (row // group) % t_steps
    kvh = row // (group * t_steps)
    head = kvh * group + row % group
    sink = jnp.zeros((1, nq, 1), F32)
    for h in range(N_KV_HEADS * group):
        sink = jnp.where(head == h, sink_ref[h], sink)
    j_h = jax.lax.broadcasted_iota(jnp.int32, (1, nq, WINDOW), 2)
    j_n = jax.lax.broadcasted_iota(jnp.int32, (1, nq, npad), 2)
    s_h = jnp.where(j_h >= t, s_h, -jnp.inf)
    s_n = jnp.where(j_n <= t, s_n, -jnp.inf)
    m = jnp.maximum(jnp.maximum(jnp.max(s_h, axis=-1, keepdims=True), jnp.max(s_n, axis=-1, keepdims=True)), sink)
    p_h = jnp.exp(s_h - m)
    p_n = jnp.exp(s_n - m)
    denom = jnp.sum(p_h, axis=-1, keepdims=True) + jnp.sum(p_n, axis=-1, keepdims=True) + jnp.exp(sink - m)
    o = (jnp.einsum('bqk,bkd->bqd', p_h.astype(BF16), hv, preferred_element_type=F32)
         + jnp.einsum('bqk,bkd->bqd', p_n.astype(BF16), nv, preferred_element_type=F32))
    lane_kvh = jax.lax.broadcasted_iota(jnp.int32, (1, nq, LANES), 2) // HEAD_DIM
    o_ref[...] = jnp.where(lane_kvh == kvh, o / denom, 0.0)


def _attn_sample(sinks, qpad, hist_k, hist_v, new_kv, *, t_steps, bb):
    n_seq, nq, _ = qpad.shape
    npad = new_kv.shape[1]
    n_heads = sinks.shape[0]
    kern = functools.partial(_attn_sample_kernel, t_steps=t_steps, group=n_heads // N_KV_HEADS)
    return pl.pallas_call(
        kern,
        grid=(n_seq // bb,),
        in_specs=[pl.BlockSpec(memory_space=pltpu.SMEM),
                  pl.BlockSpec((bb, nq, LANES), lambda i: (i, 0, 0)),
                  pl.BlockSpec((bb, WINDOW, LANES), lambda i: (i, 0, 0)),
                  pl.BlockSpec((bb, WINDOW, LANES), lambda i: (i, 0, 0)),
                  pl.BlockSpec((bb, npad, 2 * LANES), lambda i: (i, 0, 0))],
        out_specs=pl.BlockSpec((bb, nq, LANES), lambda i: (i, 0, 0)),
        out_shape=jax.ShapeDtypeStruct((n_seq, nq, LANES), F32),
        compiler_params=_params("arbitrary"),
        name="attn_sample",
    )(sinks, qpad, hist_k, hist_v, new_kv)


def _out_proj_kernel(a_ref, c_ref, x_ref, w_ref, b_ref, g_ref, beta_ref, o_ref, *, alpha):
    aw = a_ref.shape[1]
    mix = (jnp.dot(a_ref[...], w_ref[0:aw, :], preferred_element_type=F32)
           + jnp.dot(c_ref[...], w_ref[aw:, :], preferred_element_type=F32) + b_ref[...])
    o_ref[...] = _layer_norm(alpha * x_ref[...] + mix, g_ref[...], beta_ref[...])


def _out_proj(attn, c, x, w, b, g, beta, *, tm, alpha):
    rows, d = x.shape
    aw, cw = attn.shape[1], c.shape[1]
    return pl.pallas_call(
        functools.partial(_out_proj_kernel, alpha=alpha),
        grid=(rows // tm,),
        in_specs=[pl.BlockSpec((tm, aw), lambda i: (i, 0)),
                  pl.BlockSpec((tm, cw), lambda i: (i, 0)),
                  pl.BlockSpec((tm, d), lambda i: (i, 0)),
                  pl.BlockSpec((aw + cw, d), lambda i: (0, 0), pipeline_mode=pl.Buffered(1)),
                  pl.BlockSpec((1, d), lambda i: (0, 0)),
                  pl.BlockSpec((1, d), lambda i: (0, 0)),
                  pl.BlockSpec((1, d), lambda i: (0, 0))],
        out_specs=pl.BlockSpec((tm, d), lambda i: (i, 0)),
        out_shape=jax.ShapeDtypeStruct((rows, d), F32),
        compiler_params=_params("arbitrary"),
        name="out_proj",
    )(attn, c, x, w, b, g, beta)


def _ffn_kernel(x_ref, hist_ref, wup_ref, cw_ref, cb_ref, wdn_ref, g_ref, beta_ref, o_ref, hst_ref,
                xb_ref, hs_ref, p_ref, *carry, tm, prev, stride, tiles_per_seg, st0, st_rows, alpha, rb):
    i = pl.program_id(0)
    j = pl.program_id(1)
    nj = pl.num_programs(1)
    fc = p_ref.shape[1]

    @pl.when(j == 0)
    def _():
        xb_ref[...] = x_ref[...].astype(BF16)

    hs_ref[prev:prev + tm, :] = jnp.dot(xb_ref[...], wup_ref[...], preferred_element_type=F32)
    if tiles_per_seg > 1:
        carry_ref, = carry
        t = i % tiles_per_seg

        @pl.when(t == 0)
        def _():
            hs_ref[0:prev, :] = hist_ref[0]

        @pl.when(t > 0)
        def _():
            hs_ref[0:prev, :] = carry_ref[j]

        carry_ref[j] = hs_ref[tm:tm + prev, :]
    else:
        hs_ref[0:prev, :] = hist_ref[0]
    hst_ref[0] = hs_ref[prev + st0:prev + st0 + st_rows, :]

    for r0 in range(0, tm, rb):
        y = (cw_ref[2:3, :] * hs_ref[prev + r0:prev + r0 + rb, :]
             + cw_ref[1:2, :] * hs_ref[prev - stride + r0:prev - stride + r0 + rb, :]
             + cw_ref[0:1, :] * hs_ref[prev - 2 * stride + r0:prev - 2 * stride + r0 + rb, :]
             + cb_ref[...])
        yu = y[:, 0:fc]
        yg = y[:, fc:2 * fc]
        p_ref[r0:r0 + rb, :] = (yg * jax.nn.sigmoid(yg) * yu).astype(BF16)

    contrib = jnp.dot(p_ref[...], wdn_ref[...], preferred_element_type=F32)

    @pl.when(j == 0)
    def _():
        o_ref[...] = contrib

    @pl.when(j > 0)
    def _():
        o_ref[...] += contrib

    @pl.when(j == nj - 1)
    def _():
        o_ref[...] = _layer_norm(alpha * x_ref[...] + o_ref[...], g_ref[...], beta_ref[...])


def _conv_ffn(x, hist, wup, cw, cb, wdn, g, beta, *, tm, tiles_per_seg, stride, st0, st_rows, alpha):
    rows, d = x.shape
    n_seg, prev, ffw = hist.shape
    fc = FF_CHUNK
    nj = ffw // (2 * fc)
    n_tiles = rows // tm
    kern = functools.partial(_ffn_kernel, tm=tm, prev=prev, stride=stride, tiles_per_seg=tiles_per_seg,
                             st0=st0, st_rows=st_rows, alpha=alpha, rb=64 if tm % 64 == 0 else 32)
    scratch = [pltpu.VMEM((tm, d), BF16), pltpu.VMEM((prev + tm, 2 * fc), F32), pltpu.VMEM((tm, fc), BF16)]
    if tiles_per_seg > 1:
        scratch.append(pltpu.VMEM((nj, prev, 2 * fc), F32))
    return pl.pallas_call(
        kern,
        grid=(n_tiles, nj),
        in_specs=[pl.BlockSpec((tm, d), lambda i, j: (i, 0), pipeline_mode=pl.Buffered(1)),
                  pl.BlockSpec((1, prev, 2 * fc), lambda i, j: (i // tiles_per_seg, 0, j)),
                  pl.BlockSpec((d, 2 * fc), lambda i, j: (0, j)),
                  pl.BlockSpec((3, 2 * fc), lambda i, j: (0, j)),
                  pl.BlockSpec((1, 2 * fc), lambda i, j: (0, j)),
                  pl.BlockSpec((fc, d), lambda i, j: (j, 0)),
                  pl.BlockSpec((1, d), lambda i, j: (0, 0)),
                  pl.BlockSpec((1, d), lambda i, j: (0, 0))],
        out_specs=[pl.BlockSpec((tm, d), lambda i, j: (i, 0)),
                   pl.BlockSpec((1, st_rows, 2 * fc), lambda i, j: (i, 0, j))],
        out_shape=[jax.ShapeDtypeStruct((rows, d), F32),
                   jax.ShapeDtypeStruct((n_tiles, st_rows, ffw), F32)],
        scratch_shapes=scratch,
        compiler_params=_params("arbitrary", "arbitrary"),
        name="conv_ffn",
    )(x, hist, wup, cw, cb, wdn, g, beta)


def _chunk_ff(a, d_ff, d_ff_pad):
    lead = a.shape[:-1]
    a = a.reshape(lead + (2, d_ff))
    a = jnp.pad(a, [(0, 0)] * len(lead) + [(0, 0), (0, d_ff_pad - d_ff)])
    a = a.reshape(lead + (2, d_ff_pad // FF_CHUNK, FF_CHUNK))
    a = jnp.swapaxes(a, -3, -2)
    return a.reshape(lead + (2 * d_ff_pad,))


def _unchunk_ff(a, d_ff, d_ff_pad):
    lead = a.shape[:-1]
    a = a.reshape(lead + (d_ff_pad // FF_CHUNK, 2, FF_CHUNK))
    a = jnp.swapaxes(a, -3, -2).reshape(lead + (2, d_ff_pad))
    return a[..., :d_ff].reshape(lead + (2 * d_ff,))


def kernel(x_prompt, x_sample, state_attn_k, state_attn_v, state_conv, state_ffn_conv, meta_tokens, w_in, b_in, attn_sinks, conv_w, conv_b, conv_ln_g, conv_ln_b, w_out, b_out, ln1_g, ln1_b, ffn_w_up, ffn_conv_w, ffn_conv_b, ffn_w_down, ln2_g, ln2_b):
    batch, seq, d_model = x_prompt.shape
    dec_batch, dec_seq, _ = x_sample.shape
    depth = w_in.shape[0]
    conv_ch = conv_w.shape[2]
    conv_taps = conv_w.shape[1]
    ffn_taps = ffn_conv_w.shape[1]
    attn_w = d_model - conv_ch
    kv_w = 2 * N_KV_HEADS * HEAD_DIM
    n_heads = attn_w // HEAD_DIM
    group = n_heads // N_KV_HEADS
    d_ff = ffn_w_down.shape[1]
    d_ff_pad = _round_up(d_ff, FF_CHUNK)
    alpha = (2 * depth) ** 0.25
    assert ffn_taps == 3 and kv_w == 2 * LANES and dec_batch % SUBLANES == 0
    assert ffn_taps - 1 <= dec_seq <= min(conv_taps - 1, SUBLANES) and seq >= WINDOW >= conv_taps

    seq_all = N_META + seq
    lp = _round_up(seq_all, WINDOW)
    tiles_p = 6
    tm_p = lp // tiles_p
    assert tm_p % 32 == 0
    meta = jnp.broadcast_to(meta_tokens[None].astype(x_prompt.dtype), (batch, N_META, d_model))
    xp = jnp.concatenate([meta, x_prompt, jnp.zeros((batch, lp - seq_all, d_model), x_prompt.dtype)], axis=1)
    xp = xp.reshape(batch * lp, d_model)
    rows_s = dec_seq * dec_batch
    xs = jnp.swapaxes(x_sample, 0, 1).reshape(rows_s, d_model)

    conv_prev_p = 32
    ffn_prev_p = SUBLANES
    last_p = seq_all - 1
    ffn_state_tile = (last_p // tm_p)
    ffn_st0 = (last_p % tm_p) // SUBLANES * SUBLANES
    assert (last_p - 1) // tm_p == ffn_state_tile and (last_p - 1) % tm_p >= ffn_st0

    pk, pv, pc, pf, sk, sv, sc, sf = [], [], [], [], [], [], [], []
    for l in range(depth):
        wi = w_in[l]
        bi = b_in[l]
        perm = [(0, attn_w), (attn_w + kv_w, attn_w + kv_w + 2 * conv_ch), (attn_w, attn_w + kv_w)]
        wi_p = jnp.concatenate([wi[:, a:b] for a, b in perm], axis=1).astype(BF16)
        bi_p = jnp.concatenate([bi[a:b] for a, b in perm])[None]
        wo = w_out[l].astype(BF16)
        wup = _chunk_ff(ffn_w_up[l], d_ff, d_ff_pad).astype(BF16)
        fcw = _chunk_ff(ffn_conv_w[l], d_ff, d_ff_pad)
        fcb = _chunk_ff(ffn_conv_b[l], d_ff, d_ff_pad)[None]
        wdn = jnp.pad(ffn_w_down[l], ((0, d_ff_pad - d_ff), (0, 0))).astype(BF16)
        row = lambda v: v[None]
        sinks = attn_sinks[l]

        q, kv, u = _in_proj(xp, wi_p, bi_p, tm=tm_p, attn_w=attn_w, conv_ch=conv_ch, kv_w=kv_w)
        attn = _attn_prompt(sinks, q, kv, n_seg=batch, blocks_per_seg=lp // WINDOW, blk=WINDOW)
        c = _conv_module(jnp.zeros((batch, conv_prev_p, conv_ch), F32), u, conv_w[l], row(conv_b[l]),
                         row(conv_ln_g[l]), row(conv_ln_b[l]), tm=tm_p, tiles_per_seg=tiles_p, stride=1)
        x1 = _out_proj(attn, c, xp, wo, row(b_out[l]), row(ln1_g[l]), row(ln1_b[l]), tm=tm_p, alpha=alpha)
        xp, hst = _conv_ffn(x1, jnp.zeros((batch, ffn_prev_p, 2 * d_ff_pad), F32), wup, fcw, fcb, wdn,
                            row(ln2_g[l]), row(ln2_b[l]), tm=tm_p, tiles_per_seg=tiles_p, stride=1,
                            st0=ffn_st0, st_rows=SUBLANES, alpha=alpha)
        kv3 = kv.reshape(batch, lp, kv_w)[:, seq_all - WINDOW:seq_all]
        pk.append(kv3[..., :kv_w // 2].reshape(batch, WINDOW, N_KV_HEADS, HEAD_DIM))
        pv.append(kv3[..., kv_w // 2:].reshape(batch, WINDOW, N_KV_HEADS, HEAD_DIM))
        pc.append(u.reshape(batch, lp, conv_ch)[:, seq_all - (conv_taps - 1):seq_all])
        hst = hst.reshape(batch, tiles_p, SUBLANES, 2 * d_ff_pad)[:, ffn_state_tile]
        off = last_p % tm_p - ffn_st0 - (ffn_taps - 2)
        pf.append(_unchunk_ff(hst[:, off:off + ffn_taps - 1], d_ff, d_ff_pad))

        q, kv, u = _in_proj(xs, wi_p, bi_p, tm=rows_s, attn_w=attn_w, conv_ch=conv_ch, kv_w=kv_w)
        q5 = q.reshape(dec_seq, dec_batch, N_KV_HEADS, group, HEAD_DIM).transpose(1, 2, 0, 3, 4)
        q5 = q5.reshape(dec_batch, N_KV_HEADS, dec_seq * group, HEAD_DIM)
        zq = jnp.zeros_like(q5[:, 0])
        qpad = jnp.concatenate([jnp.concatenate([q5[:, 0], zq], axis=-1),
                                jnp.concatenate([zq, q5[:, 1]], axis=-1)], axis=1)
        kv_new = kv.reshape(dec_seq, dec_batch, kv_w).transpose(1, 0, 2)
        kv_new_pad = jnp.pad(kv_new, ((0, 0), (0, SUBLANES - dec_seq), (0, 0)))
        hk = state_attn_k[l].reshape(dec_batch, WINDOW, LANES)
        hv = state_attn_v[l].reshape(dec_batch, WINDOW, LANES)
        o = _attn_sample(sinks, qpad, hk, hv, kv_new_pad, t_steps=dec_seq, bb=8)
        o = (o[..., :HEAD_DIM] + o[..., HEAD_DIM:]).reshape(dec_batch, N_KV_HEADS, dec_seq, group, HEAD_DIM)
        attn = o.transpose(2, 0, 1, 3, 4).reshape(rows_s, attn_w).astype(BF16)
        c = _conv_module_sample(jnp.swapaxes(state_conv[l], 0, 1), u.reshape(dec_seq, dec_batch, conv_ch),
                                conv_w[l], row(conv_b[l]), row(conv_ln_g[l]), row(conv_ln_b[l]), bb=32)
        c = c.reshape(rows_s, conv_ch)
        x1 = _out_proj(attn, c, xs, wo, row(b_out[l]), row(ln1_g[l]), row(ln1_b[l]), tm=rows_s, alpha=alpha)
        ffn_hist = _chunk_ff(jnp.swapaxes(state_ffn_conv[l], 0, 1), d_ff, d_ff_pad)
        ffn_hist = ffn_hist.reshape(1, (ffn_taps - 1) * dec_batch, 2 * d_ff_pad)
        st_rows = (ffn_taps - 1) * dec_batch
        xs, hst = _conv_ffn(x1, ffn_hist, wup, fcw, fcb, wdn, row(ln2_g[l]), row(ln2_b[l]), tm=rows_s,
                            tiles_per_seg=1, stride=dec_batch, st0=rows_s - st_rows, st_rows=st_rows, alpha=alpha)
        k_new = kv_new[..., :kv_w // 2].reshape(dec_batch, dec_seq, N_KV_HEADS, HEAD_DIM)
        v_new = kv_new[..., kv_w // 2:].reshape(dec_batch, dec_seq, N_KV_HEADS, HEAD_DIM)
        sk.append(jnp.concatenate([state_attn_k[l][:, dec_seq:], k_new], axis=1))
        sv.append(jnp.concatenate([state_attn_v[l][:, dec_seq:], v_new], axis=1))
        u_b = u.reshape(dec_seq, dec_batch, conv_ch).transpose(1, 0, 2)
        sc.append(jnp.concatenate([state_conv[l][:, dec_seq:], u_b], axis=1))
        h_new = _unchunk_ff(hst.reshape(ffn_taps - 1, dec_batch, 2 * d_ff_pad), d_ff, d_ff_pad)
        sf.append(jnp.swapaxes(h_new, 0, 1))

    y_prompt = xp.reshape(batch, lp, d_model)[:, N_META:seq_all]
    y_sample = jnp.swapaxes(xs.reshape(dec_seq, dec_batch, d_model), 0, 1)
    return (y_prompt, y_sample, jnp.stack(pk), jnp.stack(pv), jnp.stack(pc), jnp.stack(pf),
            jnp.stack(sk), jnp.stack(sv), jnp.stack(sc), jnp.stack(sf))
```

```python
import functools

import jax
import jax.numpy as jnp
from jax.experimental import pallas as pl
from jax.experimental.pallas import tpu as pltpu

F32 = jnp.float32
BF16 = jnp.bfloat16

N_META = 16
HEAD_DIM = 64
N_KV_HEADS = 2
WINDOW = 128
LN_EPS = 1e-5
LANES = 128
SUBLANES = 8
BF16_ROWS = 16
FF_CHUNK = 512
VMEM_LIMIT = 56 * 1024 * 1024


def _round_up(x, m):
    return (x + m - 1) // m * m


def _layer_norm(y, g, b):
    mu = jnp.mean(y, axis=-1, keepdims=True)
    d = y - mu
    var = jnp.mean(d * d, axis=-1, keepdims=True)
    return d * jax.lax.rsqrt(var + LN_EPS) * g + b


def _params(*sem):
    return pltpu.CompilerParams(dimension_semantics=sem, vmem_limit_bytes=VMEM_LIMIT)


def _layer_spec(l, *block):
    zeros = (0,) * len(block)
    return pl.BlockSpec((None,) + block, lambda *_: (l,) + zeros)


def _in_proj_kernel(x_ref, w_ref, b_ref, q_ref, kv_ref, u_ref, *, attn_w, conv_ch, kv_w, q_scale):
    xb = x_ref[...].astype(BF16)

    def proj(c0, n):
        return jnp.dot(xb, w_ref[:, c0:c0 + n], preferred_element_type=F32) + b_ref[:, c0:c0 + n]

    q_ref[...] = (proj(0, attn_w) * q_scale).astype(BF16)
    kv_ref[...] = proj(attn_w, kv_w)
    a = proj(attn_w + kv_w, conv_ch)
    g = proj(attn_w + kv_w + conv_ch, conv_ch)
    u = a * jax.nn.sigmoid(g)
    for c in range(conv_ch // LANES):
        u_ref[c] = u[:, c * LANES:(c + 1) * LANES]


def _in_proj(x, w, b, l, *, tm, attn_w, conv_ch, kv_w):
    rows, d = x.shape
    n = w.shape[2]
    ct = conv_ch // LANES
    kern = functools.partial(_in_proj_kernel, attn_w=attn_w, conv_ch=conv_ch, kv_w=kv_w,
                             q_scale=HEAD_DIM ** -0.5)
    return pl.pallas_call(
        kern,
        grid=(rows // tm,),
        in_specs=[pl.BlockSpec((tm, d), lambda i: (i, 0)),
                  pl.BlockSpec((None, d, n), lambda i: (l, 0, 0), pipeline_mode=pl.Buffered(1)),
                  _layer_spec(l, 1, n)],
        out_specs=[pl.BlockSpec((tm, attn_w), lambda i: (i, 0)),
                   pl.BlockSpec((tm, kv_w), lambda i: (i, 0)),
                   pl.BlockSpec((ct, tm, LANES), lambda i: (0, i, 0))],
        out_shape=[jax.ShapeDtypeStruct((rows, attn_w), BF16),
                   jax.ShapeDtypeStruct((rows, kv_w), F32),
                   jax.ShapeDtypeStruct((ct, rows, LANES), F32)],
        compiler_params=_params("arbitrary"),
        name="in_proj",
    )(x, w, b)


def _swish_ln_tiles(tiles, g_ref, b_ref, ch):
    total = tiles[0]
    for tl in tiles[1:]:
        total = total + tl
    mu = jnp.sum(total, axis=-1, keepdims=True) / ch
    ds = [tl - mu for tl in tiles]
    sq = ds[0] * ds[0]
    for d in ds[1:]:
        sq = sq + d * d
    inv = jax.lax.rsqrt(jnp.sum(sq, axis=-1, keepdims=True) / ch + LN_EPS)
    out = []
    for c, d in enumerate(ds):
        y = d * inv * g_ref[:, c * LANES:(c + 1) * LANES] + b_ref[:, c * LANES:(c + 1) * LANES]
        out.append((y * jax.nn.sigmoid(y)).astype(BF16))
    return out


def _conv_kernel(hist_ref, u_ref, w_ref, cb_ref, g_ref, b_ref, c_ref, ext_ref, tail_ref, y_ref, *, tm, taps, qb):
    ct = u_ref.shape[0]
    ch = ct * LANES
    nq = tm // SUBLANES
    nb = taps - 1
    hb = hist_ref.shape[2]
    t = pl.program_id(1)

    @pl.when(t == 0)
    def _():
        for c in range(ct):
            for e in range(nb):
                row = hist_ref[0, c, hb - nb + e:hb - nb + e + 1, :]
                tail_ref[c, e * SUBLANES:(e + 1) * SUBLANES, :] = jnp.broadcast_to(row, (SUBLANES, LANES))

    @pl.when(t > 0)
    def _():
        tail_ref[...] = ext_ref[:, nq * SUBLANES:(nq + nb) * SUBLANES, :]

    def load(q, carry):
        dst = pl.multiple_of((nb + q) * SUBLANES, SUBLANES)
        for c in range(ct):
            ext_ref[c, pl.ds(dst, SUBLANES), :] = u_ref[c, pl.ds(q, SUBLANES, stride=nq), :]
        return carry

    jax.lax.fori_loop(0, nq, load, 0)

    first_strip = jax.lax.broadcasted_iota(jnp.int32, (SUBLANES, LANES), 0) == 0
    for c in range(ct):
        for e in range(nb):
            cur = ext_ref[c, (nq + e) * SUBLANES:(nq + e + 1) * SUBLANES, :]
            prv = tail_ref[c, e * SUBLANES:(e + 1) * SUBLANES, :]
            ext_ref[c, e * SUBLANES:(e + 1) * SUBLANES, :] = jnp.where(
                first_strip, pltpu.roll(prv, 1, 0), pltpu.roll(cur, 1, 0))

    def conv_block(ib, carry):
        rows = qb * SUBLANES
        for c in range(ct):
            lanes = slice(c * LANES, (c + 1) * LANES)
            acc = w_ref[0:1, lanes] * ext_ref[c, pl.ds(pl.multiple_of(ib * rows, rows), rows), :]
            for j in range(1, taps):
                src = pl.multiple_of(ib * rows + j * SUBLANES, SUBLANES)
                acc = acc + w_ref[j:j + 1, lanes] * ext_ref[c, pl.ds(src, rows), :]
            y_ref[c, pl.ds(pl.multiple_of(ib * rows, rows), rows), :] = acc + cb_ref[:, lanes]
        return carry

    jax.lax.fori_loop(0, nq // qb, conv_block, 0)

    for r0 in range(0, tm, BF16_ROWS):
        tiles = []
        for c in range(ct):
            halves = []
            for rr in range(r0, r0 + BF16_ROWS, SUBLANES):
                s, q = divmod(rr, nq)
                halves.append(y_ref[c, pl.ds(q * SUBLANES + s, SUBLANES, stride=SUBLANES), :])
            tiles.append(jnp.concatenate(halves, axis=0))
        for c, o in enumerate(_swish_ln_tiles(tiles, g_ref, b_ref, ch)):
            c_ref[r0:r0 + BF16_ROWS, c * LANES:(c + 1) * LANES] = o


def _conv_module(hist, u, w, cb, g, b, l, *, tm, tiles_per_seg):
    ct, rows, _ = u.shape
    ch = ct * LANES
    n_seg, _, hb, _ = hist.shape
    taps = w.shape[1]
    nq = tm // SUBLANES
    qb = 8
    assert nq % qb == 0 and hb >= taps - 1 and nq >= taps - 1
    kern = functools.partial(_conv_kernel, tm=tm, taps=taps, qb=qb)
    return pl.pallas_call(
        kern,
        grid=(n_seg, tiles_per_seg),
        in_specs=[pl.BlockSpec((1, ct, hb, LANES), lambda s, t: (s, 0, 0, 0)),
                  pl.BlockSpec((ct, tm, LANES), lambda s, t: (0, s * tiles_per_seg + t, 0)),
                  _layer_spec(l, taps, ch), _layer_spec(l, 1, ch), _layer_spec(l, 1, ch), _layer_spec(l, 1, ch)],
        out_specs=pl.BlockSpec((tm, ch), lambda s, t: (s * tiles_per_seg + t, 0)),
        out_shape=jax.ShapeDtypeStruct((rows, ch), BF16),
        scratch_shapes=[pltpu.VMEM((ct, tm + (taps - 1) * SUBLANES, LANES), F32),
                        pltpu.VMEM((ct, (taps - 1) * SUBLANES, LANES), F32),
                        pltpu.VMEM((ct, tm, LANES), F32)],
        compiler_params=_params("arbitrary", "arbitrary"),
        name="conv_module",
    )(hist, u, w, cb, g, b)


def _conv_sample_kernel(hist_ref, u_ref, w_ref, cb_ref, g_ref, b_ref, c_ref):
    n_hist = hist_ref.shape[0]
    ct, t_steps = u_ref.shape[0], u_ref.shape[1]
    for t in range(t_steps):
        tiles = []
        for c in range(ct):
            lanes = slice(c * LANES, (c + 1) * LANES)
            acc = None
            for j in range(w_ref.shape[0]):
                s = t + j
                src = hist_ref[s, :, lanes] if s < n_hist else u_ref[c, s - n_hist]
                term = w_ref[j:j + 1, lanes] * src
                acc = term if acc is None else acc + term
            tiles.append(acc + cb_ref[:, lanes])
        for c, o in enumerate(_swish_ln_tiles(tiles, g_ref, b_ref, ct * LANES)):
            c_ref[t, :, c * LANES:(c + 1) * LANES] = o


def _conv_module_sample(hist, u, w, cb, g, b, l, *, bb):
    ct, t_steps, n_seq, _ = u.shape
    ch = ct * LANES
    n_hist = hist.shape[0]
    taps = w.shape[1]
    assert n_hist == taps - 1
    return pl.pallas_call(
        _conv_sample_kernel,
        grid=(n_seq // bb,),
        in_specs=[pl.BlockSpec((n_hist, bb, ch), lambda i: (0, i, 0)),
                  pl.BlockSpec((ct, t_steps, bb, LANES), lambda i: (0, 0, i, 0)),
                  _layer_spec(l, taps, ch), _layer_spec(l, 1, ch), _layer_spec(l, 1, ch), _layer_spec(l, 1, ch)],
        out_specs=pl.BlockSpec((t_steps, bb, ch), lambda i: (0, i, 0)),
        out_shape=jax.ShapeDtypeStruct((t_steps, n_seq, ch), BF16),
        compiler_params=_params("arbitrary"),
        name="conv_module_sample",
    )(hist, u, w, cb, g, b)


def _attn_prompt_kernel(sink_ref, q_ref, kvp_ref, kvc_ref, o_ref, *, n_heads, group):
    blk = q_ref.shape[0]
    i = pl.program_id(1)
    kv = jnp.concatenate([kvp_ref[...], kvc_ref[...]], axis=0)
    kband = kv[:, 0:LANES]
    vband = kv[:, LANES:2 * LANES]
    kroll = pltpu.roll(kband, HEAD_DIM, 1)
    vroll = pltpu.roll(vband, HEAD_DIM, 1)
    lane = jax.lax.broadcasted_iota(jnp.int32, (2 * blk, LANES), 1)
    lo = lane < HEAD_DIM

    def lo_hi(x_lo, x_hi):
        return jnp.concatenate([jnp.where(lo, x_lo, 0.0), jnp.where(lo, 0.0, x_hi)], axis=0).astype(BF16)

    kab = [lo_hi(kband, kroll), lo_hi(kroll, kband)]
    vab = [lo_hi(vband, vroll), lo_hi(vroll, vband)]

    r = jax.lax.broadcasted_iota(jnp.int32, (blk, 2 * blk), 0)
    c = jax.lax.broadcasted_iota(jnp.int32, (blk, 2 * blk), 1)
    mask = (c >= r) & (c <= r + WINDOW) & ((c >= blk) | (i > 0))
    out_lo = jax.lax.broadcasted_iota(jnp.int32, (blk, LANES), 1) < HEAD_DIM

    for pair in range(n_heads // 2):
        kvh = (2 * pair) // group
        q2 = q_ref[:, pair * LANES:(pair + 1) * LANES]
        s2 = jax.lax.dot_general(q2, kab[kvh], (((1,), (1,)), ((), ())), preferred_element_type=F32)
        ps, invs = [], []
        for half in range(2):
            sink = sink_ref[2 * pair + half]
            s = jnp.where(mask, s2[:, half * 2 * blk:(half + 1) * 2 * blk], -jnp.inf)
            m = jnp.maximum(jnp.max(s, axis=-1, keepdims=True), sink)
            p = jnp.exp(s - m)
            denom = jnp.sum(p, axis=-1, keepdims=True) + jnp.exp(sink - m)
            ps.append(p.astype(BF16))
            invs.append(1.0 / denom)
        o2 = jnp.dot(jnp.concatenate(ps, axis=1), vab[kvh], preferred_element_type=F32)
        o2 = o2 * jnp.where(out_lo, invs[0], invs[1])
        o_ref[:, pair * LANES:(pair + 1) * LANES] = o2.astype(BF16)


def _attn_prompt(sinks, q, kv, *, n_seg, blocks_per_seg, blk):
    rows, attn_w = q.shape
    kv_w = kv.shape[1]
    n_heads = attn_w // HEAD_DIM
    kern = functools.partial(_attn_prompt_kernel, n_heads=n_heads, group=n_heads // N_KV_HEADS)
    return pl.pallas_call(
        kern,
        grid=(n_seg, blocks_per_seg),
        in_specs=[pl.BlockSpec(memory_space=pltpu.SMEM),
                  pl.BlockSpec((blk, attn_w), lambda s, i: (s * blocks_per_seg + i, 0)),
                  pl.BlockSpec((blk, kv_w), lambda s, i: (jnp.maximum(s * blocks_per_seg + i - 1, 0), 0)),
                  pl.BlockSpec((blk, kv_w), lambda s, i: (s * blocks_per_seg + i, 0))],
        out_specs=pl.BlockSpec((blk, attn_w), lambda s, i: (s * blocks_per_seg + i, 0)),
        out_shape=jax.ShapeDtypeStruct((rows, attn_w), BF16),
        compiler_params=_params("arbitrary", "arbitrary"),
        name="attn_prompt",
    )(sinks, q, kv, kv)


def _attn_sample_kernel(sink_ref, q_ref, hk_ref, hv_ref, nkv_ref, o_ref, *, t_steps, group):
    bb, nq, _ = q_ref.shape
    npad = nkv_ref.shape[1]
    q = q_ref[...]
    hk = hk_ref[...].astype(BF16)
    hv = hv_ref[...].astype(BF16)
    nk = nkv_ref[:, :, 0:LANES].astype(BF16)
    nv = nkv_ref[:, :, LANES:2 * LANES].astype(BF16)
    s_h = jnp.einsum('bqd,bkd->bqk', q, hk, preferred_element_type=F32)
    s_n = jnp.einsum('bqd,bkd->bqk', q, nk, preferred_element_type=F32)
    row = jax.lax.broadcasted_iota(jnp.int32, (1, nq, 1), 1)
    t = (row // group) % t_steps
    kvh = row // (group * t_steps)
    head = kvh * group + row % group
    sink = jnp.zeros((1, nq, 1), F32)
    for h in range(N_KV_HEADS * group):
        sink = jnp.where(head == h, sink_ref[h], sink)
    j_h = jax.lax.broadcasted_iota(jnp.int32, (1, nq, WINDOW), 2)
    j_n = jax.lax.broadcasted_iota(jnp.int32, (1, nq, npad), 2)
    s_h = jnp.where(j_h >= t, s_h, -jnp.inf)
    s_n = jnp.where(j_n <= t, s_n, -jnp.inf)
    m = jnp.maximum(jnp.maximum(jnp.max(s_h, axis=-1, keepdims=True), jnp.max(s_n, axis=-1, keepdims=True)), sink)
    p_h = jnp.exp(s_h - m)
    p_n = jnp.exp(s_n - m)
    denom = jnp.sum(p_h, axis=-1, keepdims=True) + jnp.sum(p_n, axis=-1, keepdims=True) + jnp.exp(sink - m)
    o = (jnp.einsum('bqk,bkd->bqd', p_h.astype(BF16), hv, preferred_element_type=F32)
         + jnp.einsum('bqk,bkd->bqd', p_n.astype(BF16), nv, preferred_element_type=F32))
    lane_kvh = jax.lax.broadcasted_iota(jnp.int32, (1, nq, LANES), 2) // HEAD_DIM
    o_ref[...] = jnp.where(lane_kvh == kvh, o / denom, 0.0)


def _attn_sample(sinks, qpad, hist_k, hist_v, new_kv, *, t_steps, bb):
    n_seq, nq, _ = qpad.shape
    npad = new_kv.shape[1]
    n_heads = sinks.shape[0]
    kern = functools.partial(_attn_sample_kernel, t_steps=t_steps, group=n_heads // N_KV_HEADS)
    return pl.pallas_call(
        kern,
        grid=(n_seq // bb,),
        in_specs=[pl.BlockSpec(memory_space=pltpu.SMEM),
                  pl.BlockSpec((bb, nq, LANES), lambda i: (i, 0, 0)),
                  pl.BlockSpec((bb, WINDOW, LANES), lambda i: (i, 0, 0)),
                  pl.BlockSpec((bb, WINDOW, LANES), lambda i: (i, 0, 0)),
                  pl.BlockSpec((bb, npad, 2 * LANES), lambda i: (i, 0, 0))],
        out_specs=pl.BlockSpec((bb, nq, LANES), lambda i: (i, 0, 0)),
        out_shape=jax.ShapeDtypeStruct((n_seq, nq, LANES), F32),
        compiler_params=_params("arbitrary"),
        name="attn_sample",
    )(sinks, qpad, hist_k, hist_v, new_kv)


def _out_proj_kernel(a_ref, c_ref, x_ref, w_ref, b_ref, g_ref, beta_ref, o_ref, *, alpha):
    aw = a_ref.shape[1]
    mix = (jnp.dot(a_ref[...], w_ref[0:aw, :], preferred_element_type=F32)
           + jnp.dot(c_ref[...], w_ref[aw:, :], preferred_element_type=F32) + b_ref[...])
    o_ref[...] = _layer_norm(alpha * x_ref[...] + mix, g_ref[...], beta_ref[...])


def _out_proj(attn, c, x, w, b, g, beta, l, *, tm, alpha):
    rows, d = x.shape
    aw, cw = attn.shape[1], c.shape[1]
    return pl.pallas_call(
        functools.partial(_out_proj_kernel, alpha=alpha),
        grid=(rows // tm,),
        in_specs=[pl.BlockSpec((tm, aw), lambda i: (i, 0)),
                  pl.BlockSpec((tm, cw), lambda i: (i, 0)),
                  pl.BlockSpec((tm, d), lambda i: (i, 0)),
                  pl.BlockSpec((None, aw + cw, d), lambda i: (l, 0, 0), pipeline_mode=pl.Buffered(1)),
                  _layer_spec(l, 1, d), _layer_spec(l, 1, d), _layer_spec(l, 1, d)],
        out_specs=pl.BlockSpec((tm, d), lambda i: (i, 0)),
        out_shape=jax.ShapeDtypeStruct((rows, d), F32),
        compiler_params=_params("arbitrary"),
        name="out_proj",
    )(attn, c, x, w, b, g, beta)


def _ffn_kernel(x_ref, hu_ref, hg_ref, wu_ref, wg_ref, cwu_ref, cwg_ref, cbu_ref, cbg_ref, wdn_ref, g_ref, beta_ref,
                o_ref, su_ref, sg_ref, xb_ref, hsu_ref, hsg_ref, p_ref, *carry,
                tm, prev, stride, tiles_per_seg, st0, st_rows, alpha, rb, n_split, dup_cols):
    i = pl.program_id(0)
    j = pl.program_id(1)
    nj = pl.num_programs(1)
    fc = p_ref.shape[1]
    hm = tm // n_split

    @pl.when(j == 0)
    def _():
        xb_ref[...] = x_ref[...].astype(BF16)
        o_ref[...] = jnp.zeros_like(o_ref)
        if carry:
            @pl.when(i == 0)
            def _():
                for cr in carry:
                    cr[...] = jnp.zeros_like(cr)

    for h in range(n_split):
        xs = xb_ref[h * hm:(h + 1) * hm, :]
        hsu_ref[prev + h * hm:prev + (h + 1) * hm, :] = jnp.dot(xs, wu_ref[0], preferred_element_type=F32)
        hsg_ref[prev + h * hm:prev + (h + 1) * hm, :] = jnp.dot(xs, wg_ref[0], preferred_element_type=F32)

    if carry:
        first = (i % tiles_per_seg) == 0
        for hs_ref, hist_ref, cr in ((hsu_ref, hu_ref, carry[0]), (hsg_ref, hg_ref, carry[1])):
            hs_ref[0:prev, :] = jnp.where(first, hist_ref[0], cr[j])
            cr[j] = hs_ref[tm:tm + prev, :]
    else:
        hsu_ref[0:prev, :] = hu_ref[0]
        hsg_ref[0:prev, :] = hg_ref[0]
    for st_ref, hs_ref in ((su_ref, hsu_ref), (sg_ref, hsg_ref)):
        v = hs_ref[prev + st0:prev + st0 + st_rows, :]
        if dup_cols:
            shifted = jnp.concatenate([v[:, dup_cols:], jnp.zeros((st_rows, dup_cols), F32)], axis=1)
            v = jnp.where(j == nj - 1, shifted, v)
        st_ref[0] = v

    def conv(hs_ref, cw_ref, cb_ref, r0):
        return (cw_ref[0, 2:3, :] * hs_ref[prev + r0:prev + r0 + rb, :]
                + cw_ref[0, 1:2, :] * hs_ref[prev - stride + r0:prev - stride + r0 + rb, :]
                + cw_ref[0, 0:1, :] * hs_ref[prev - 2 * stride + r0:prev - 2 * stride + r0 + rb, :]
                + cb_ref[0])

    col = jax.lax.broadcasted_iota(jnp.int32, (rb, fc), 1)
    keep = (col >= dup_cols) | (j < nj - 1)
    for h in range(n_split):
        for r0 in range(h * hm, (h + 1) * hm, rb):
            yu = conv(hsu_ref, cwu_ref, cbu_ref, r0)
            yg = conv(hsg_ref, cwg_ref, cbg_ref, r0)
            p_ref[r0:r0 + rb, :] = jnp.where(keep, yg * jax.nn.sigmoid(yg) * yu, 0.0).astype(BF16)
        rows = slice(h * hm, (h + 1) * hm)
        o_ref[rows, :] += jnp.dot(p_ref[rows, :], wdn_ref[0], preferred_element_type=F32)

    @pl.when(j == nj - 1)
    def _():
        o_ref[...] = _layer_norm(alpha * x_ref[...] + o_ref[...], g_ref[...], beta_ref[...])


def _conv_ffn(x, hist, wup, cw, cb, wdn, g, beta, l, *, tm, tiles_per_seg, stride, st0, st_rows, alpha):
    rows, d = x.shape
    n_seg, prev, _ = hist.shape
    d_ff = wdn.shape[1]
    fc = FF_CHUNK
    nblk = d_ff // LANES
    cblk = fc // LANES
    nj = pl.cdiv(d_ff, fc)
    dup_cols = nj * fc - d_ff
    n_tiles = rows // tm
    n_split = 2
    hm = tm // n_split
    rb = 32
    assert d_ff % LANES == 0 and tm % n_split == 0 and hm % rb == 0 and nblk >= cblk

    def blk0(j):
        return jnp.minimum(j * cblk, nblk - cblk)

    el = pl.Element
    col_u = lambda j: LANES * blk0(j)
    col_g = lambda j: LANES * (nblk + blk0(j))
    kern = functools.partial(_ffn_kernel, tm=tm, prev=prev, stride=stride, tiles_per_seg=tiles_per_seg,
                             st0=st0, st_rows=st_rows, alpha=alpha, rb=rb, n_split=n_split, dup_cols=dup_cols)
    scratch = [pltpu.VMEM((tm, d), BF16), pltpu.VMEM((prev + tm, fc), F32), pltpu.VMEM((prev + tm, fc), F32),
               pltpu.VMEM((tm, fc), BF16)]
    if tiles_per_seg > 1:
        scratch += [pltpu.VMEM((nj, prev, fc), F32), pltpu.VMEM((nj, prev, fc), F32)]
    hist_blk = (el(1), el(prev), el(fc))
    st_spec = pl.BlockSpec((1, st_rows, fc), lambda i, j: (i, 0, j))
    return pl.pallas_call(
        kern,
        grid=(n_tiles, nj),
        in_specs=[pl.BlockSpec((tm, d), lambda i, j: (i, 0), pipeline_mode=pl.Buffered(1)),
                  pl.BlockSpec(hist_blk, lambda i, j: (i // tiles_per_seg, 0, col_u(j))),
                  pl.BlockSpec(hist_blk, lambda i, j: (i // tiles_per_seg, 0, col_g(j))),
                  pl.BlockSpec((el(1), el(d), el(fc)), lambda i, j: (l, 0, col_u(j))),
                  pl.BlockSpec((el(1), el(d), el(fc)), lambda i, j: (l, 0, col_g(j))),
                  pl.BlockSpec((el(1), el(3), el(fc)), lambda i, j: (l, 0, col_u(j))),
                  pl.BlockSpec((el(1), el(3), el(fc)), lambda i, j: (l, 0, col_g(j))),
                  pl.BlockSpec((el(1), el(1), el(fc)), lambda i, j: (l, 0, col_u(j))),
                  pl.BlockSpec((el(1), el(1), el(fc)), lambda i, j: (l, 0, col_g(j))),
                  pl.BlockSpec((el(1), el(fc), el(d)), lambda i, j: (l, col_u(j), 0)),
                  _layer_spec(l, 1, d), _layer_spec(l, 1, d)],
        out_specs=[pl.BlockSpec((tm, d), lambda i, j: (i, 0)), st_spec, st_spec],
        out_shape=[jax.ShapeDtypeStruct((rows, d), F32),
                   jax.ShapeDtypeStruct((n_tiles, st_rows, nj * fc), F32),
                   jax.ShapeDtypeStruct((n_tiles, st_rows, nj * fc), F32)],
        scratch_shapes=scratch,
        compiler_params=_params("arbitrary", "arbitrary"),
        name="conv_ffn",
    )(x, hist, hist, wup, wup, cw, cw, cb, cb, wdn, g, beta)


def kernel(x_prompt, x_sample, state_attn_k, state_attn_v, state_conv, state_ffn_conv, meta_tokens, w_in, b_in, attn_sinks, conv_w, conv_b, conv_ln_g, conv_ln_b, w_out, b_out, ln1_g, ln1_b, ffn_w_up, ffn_conv_w, ffn_conv_b, ffn_w_down, ln2_g, ln2_b):
    batch, seq, d_model = x_prompt.shape
    dec_batch, dec_seq, _ = x_sample.shape
    depth = w_in.shape[0]
    conv_ch = conv_w.shape[2]
    conv_taps = conv_w.shape[1]
    ffn_taps = ffn_conv_w.shape[1]
    attn_w = d_model - conv_ch
    kv_w = 2 * N_KV_HEADS * HEAD_DIM
    n_heads = attn_w // HEAD_DIM
    group = n_heads // N_KV_HEADS
    d_ff = ffn_w_down.shape[1]
    ct = conv_ch // LANES
    alpha = (2 * depth) ** 0.25
    assert ffn_taps == 3 and kv_w == 2 * LANES and dec_batch % SUBLANES == 0
    assert ffn_taps - 1 <= dec_seq <= min(conv_taps - 1, SUBLANES) and seq >= WINDOW >= conv_taps

    seq_all = N_META + seq
    lp = _round_up(seq_all, WINDOW)
    tiles_p = 6
    tm_p = lp // tiles_p
    assert tm_p % 64 == 0
    meta = jnp.broadcast_to(meta_tokens[None].astype(x_prompt.dtype), (batch, N_META, d_model))
    xp = jnp.concatenate([meta, x_prompt, jnp.zeros((batch, lp - seq_all, d_model), x_prompt.dtype)], axis=1)
    xp = xp.reshape(batch * lp, d_model)
    rows_s = dec_seq * dec_batch
    xs = jnp.swapaxes(x_sample, 0, 1).reshape(rows_s, d_model)

    conv_hist_rows = 32
    ffn_prev_p = SUBLANES
    last_p = seq_all - 1
    ffn_state_tile = last_p // tm_p
    ffn_st0 = (last_p % tm_p) // SUBLANES * SUBLANES
    assert (last_p - 1) // tm_p == ffn_state_tile and (last_p - 1) % tm_p >= ffn_st0

    w_in_b, w_out_b = w_in.astype(BF16), w_out.astype(BF16)
    w_up_b, w_dn_b = ffn_w_up.astype(BF16), ffn_w_down.astype(BF16)
    rows3 = lambda v: v[:, None, :]
    b_in3, conv_b3, cg3, cbt3 = rows3(b_in), rows3(conv_b), rows3(conv_ln_g), rows3(conv_ln_b)
    b_out3, g1, be1, g2, be2 = rows3(b_out), rows3(ln1_g), rows3(ln1_b), rows3(ln2_g), rows3(ln2_b)
    fcb3 = rows3(ffn_conv_b)
    zero_conv_hist = jnp.zeros((batch, ct, conv_hist_rows, LANES), F32)
    zero_ffn_hist = jnp.zeros((batch, ffn_prev_p, 2 * d_ff), F32)

    pk, pv, pc, pf, sk, sv, sc, sf = [], [], [], [], [], [], [], []
    for l in range(depth):
        sinks = attn_sinks[l]

        q, kv, u = _in_proj(xp, w_in_b, b_in3, l, tm=tm_p, attn_w=attn_w, conv_ch=conv_ch, kv_w=kv_w)
        attn = _attn_prompt(sinks, q, kv, n_seg=batch, blocks_per_seg=lp // WINDOW, blk=WINDOW)
        c = _conv_module(zero_conv_hist, u, conv_w, conv_b3, cg3, cbt3, l, tm=tm_p, tiles_per_seg=tiles_p)
        x1 = _out_proj(attn, c, xp, w_out_b, b_out3, g1, be1, l, tm=tm_p, alpha=alpha)
        xp, su, sg = _conv_ffn(x1, zero_ffn_hist, w_up_b, ffn_conv_w, fcb3, w_dn_b, g2, be2, l, tm=tm_p,
                               tiles_per_seg=tiles_p, stride=1, st0=ffn_st0, st_rows=SUBLANES, alpha=alpha)
        kv3 = kv.reshape(batch, lp, kv_w)[:, seq_all - WINDOW:seq_all]
        pk.append(kv3[..., :kv_w // 2].reshape(batch, WINDOW, N_KV_HEADS, HEAD_DIM))
        pv.append(kv3[..., kv_w // 2:].reshape(batch, WINDOW, N_KV_HEADS, HEAD_DIM))
        u_tail = u.reshape(ct, batch, lp, LANES)[:, :, seq_all - (conv_taps - 1):seq_all]
        pc.append(u_tail.transpose(1, 2, 0, 3).reshape(batch, conv_taps - 1, conv_ch))
        off = last_p % tm_p - ffn_st0 - (ffn_taps - 2)
        hst = jnp.concatenate([su[..., :d_ff], sg[..., :d_ff]], axis=-1).reshape(batch, tiles_p, SUBLANES, 2 * d_ff)
        pf.append(hst[:, ffn_state_tile, off:off + ffn_taps - 1])

        q, kv, u = _in_proj(xs, w_in_b, b_in3, l, tm=rows_s, attn_w=attn_w, conv_ch=conv_ch, kv_w=kv_w)
        q5 = q.reshape(dec_seq, dec_batch, N_KV_HEADS, group, HEAD_DIM).transpose(1, 2, 0, 3, 4)
        q5 = q5.reshape(dec_batch, N_KV_HEADS, dec_seq * group, HEAD_DIM)
        zq = jnp.zeros_like(q5[:, 0])
        qpad = jnp.concatenate([jnp.concatenate([q5[:, 0], zq], axis=-1),
                                jnp.concatenate([zq, q5[:, 1]], axis=-1)], axis=1)
        kv_new = kv.reshape(dec_seq, dec_batch, kv_w).transpose(1, 0, 2)
        kv_new_pad = jnp.pad(kv_new, ((0, 0), (0, SUBLANES - dec_seq), (0, 0)))
        hk = state_attn_k[l].reshape(dec_batch, WINDOW, LANES)
        hv = state_attn_v[l].reshape(dec_batch, WINDOW, LANES)
        o = _attn_sample(sinks, qpad, hk, hv, kv_new_pad, t_steps=dec_seq, bb=8)
        o = (o[..., :HEAD_DIM] + o[..., HEAD_DIM:]).reshape(dec_batch, N_KV_HEADS, dec_seq, group, HEAD_DIM)
        attn = o.transpose(2, 0, 1, 3, 4).reshape(rows_s, attn_w).astype(BF16)
        u4 = u.reshape(ct, dec_seq, dec_batch, LANES)
        c = _conv_module_sample(jnp.swapaxes(state_conv[l], 0, 1), u4, conv_w, conv_b3, cg3, cbt3, l, bb=32)
        c = c.reshape(rows_s, conv_ch)
        x1 = _out_proj(attn, c, xs, w_out_b, b_out3, g1, be1, l, tm=rows_s, alpha=alpha)
        st_rows = (ffn_taps - 1) * dec_batch
        ffn_hist = jnp.swapaxes(state_ffn_conv[l], 0, 1).reshape(1, st_rows, 2 * d_ff)
        xs, su, sg = _conv_ffn(x1, ffn_hist, w_up_b, ffn_conv_w, fcb3, w_dn_b, g2, be2, l, tm=rows_s,
                               tiles_per_seg=1, stride=dec_batch, st0=rows_s - st_rows, st_rows=st_rows,
                               alpha=alpha)
        k_new = kv_new[..., :kv_w // 2].reshape(dec_batch, dec_seq, N_KV_HEADS, HEAD_DIM)
        v_new = kv_new[..., kv_w // 2:].reshape(dec_batch, dec_seq, N_KV_HEADS, HEAD_DIM)
        sk.append(jnp.concatenate([state_attn_k[l][:, dec_seq:], k_new], axis=1))
        sv.append(jnp.concatenate([state_attn_v[l][:, dec_seq:], v_new], axis=1))
        u_b = u4.transpose(2, 1, 0, 3).reshape(dec_batch, dec_seq, conv_ch)
        sc.append(jnp.concatenate([state_conv[l][:, dec_seq:], u_b], axis=1))
        h_new = jnp.concatenate([su[..., :d_ff], sg[..., :d_ff]], axis=-1).reshape(ffn_taps - 1, dec_batch, 2 * d_ff)
        sf.append(jnp.swapaxes(h_new, 0, 1))

    y_prompt = xp.reshape(batch, lp, d_model)[:, N_META:seq_all]
    y_sample = jnp.swapaxes(xs.reshape(dec_seq, dec_batch, d_model), 0, 1)
    return (y_prompt, y_sample, jnp.stack(pk), jnp.stack(pv), jnp.stack(pc), jnp.stack(pf),
            jnp.stack(sk), jnp.stack(sv), jnp.stack(sc), jnp.stack(sf))
```

```python
import functools

import jax
import jax.numpy as jnp
from jax.experimental import pallas as pl
from jax.experimental.pallas import tpu as pltpu

F32 = jnp.float32
BF16 = jnp.bfloat16

N_META = 16
HEAD_DIM = 64
N_KV_HEADS = 2
WINDOW = 128
LN_EPS = 1e-5
LANES = 128
SUBLANES = 8
BF16_ROWS = 16
FF_CHUNK = 512
VMEM_LIMIT = 56 * 1024 * 1024


def _round_up(x, m):
    return (x + m - 1) // m * m


def _layer_norm(y, g, b):
    mu = jnp.mean(y, axis=-1, keepdims=True)
    d = y - mu
    var = jnp.mean(d * d, axis=-1, keepdims=True)
    return d * jax.lax.rsqrt(var + LN_EPS) * g + b


def _params(*sem):
    return pltpu.CompilerParams(dimension_semantics=sem, vmem_limit_bytes=VMEM_LIMIT)


def _layer_spec(l, *block):
    zeros = (0,) * len(block)
    return pl.BlockSpec((None,) + block, lambda *_: (l,) + zeros)


def _in_proj_kernel(x_ref, w_ref, b_ref, q_ref, kv_ref, u_ref, *, attn_w, conv_ch, kv_w, q_scale):
    xb = x_ref[...].astype(BF16)

    def proj(c0, n):
        return jnp.dot(xb, w_ref[:, c0:c0 + n], preferred_element_type=F32) + b_ref[:, c0:c0 + n]

    q_ref[...] = (proj(0, attn_w) * q_scale).astype(BF16)
    kv_ref[...] = proj(attn_w, kv_w)
    a = proj(attn_w + kv_w, conv_ch)
    g = proj(attn_w + kv_w + conv_ch, conv_ch)
    u = a * jax.nn.sigmoid(g)
    for c in range(conv_ch // LANES):
        u_ref[c] = u[:, c * LANES:(c + 1) * LANES]


def _in_proj(x, w, b, l, *, tm, attn_w, conv_ch, kv_w):
    rows, d = x.shape
    n = w.shape[2]
    ct = conv_ch // LANES
    kern = functools.partial(_in_proj_kernel, attn_w=attn_w, conv_ch=conv_ch, kv_w=kv_w,
                             q_scale=HEAD_DIM ** -0.5)
    return pl.pallas_call(
        kern,
        grid=(rows // tm,),
        in_specs=[pl.BlockSpec((tm, d), lambda i: (i, 0)),
                  pl.BlockSpec((None, d, n), lambda i: (l, 0, 0), pipeline_mode=pl.Buffered(1)),
                  _layer_spec(l, 1, n)],
        out_specs=[pl.BlockSpec((tm, attn_w), lambda i: (i, 0)),
                   pl.BlockSpec((tm, kv_w), lambda i: (i, 0)),
                   pl.BlockSpec((ct, tm, LANES), lambda i: (0, i, 0))],
        out_shape=[jax.ShapeDtypeStruct((rows, attn_w), BF16),
                   jax.ShapeDtypeStruct((rows, kv_w), F32),
                   jax.ShapeDtypeStruct((ct, rows, LANES), F32)],
        compiler_params=_params("arbitrary"),
        name="in_proj",
    )(x, w, b)


def _swish_ln_tiles(tiles, g_ref, b_ref, ch):
    total = tiles[0]
    for tl in tiles[1:]:
        total = total + tl
    mu = jnp.sum(total, axis=-1, keepdims=True) / ch
    ds = [tl - mu for tl in tiles]
    sq = ds[0] * ds[0]
    for d in ds[1:]:
        sq = sq + d * d
    inv = jax.lax.rsqrt(jnp.sum(sq, axis=-1, keepdims=True) / ch + LN_EPS)
    out = []
    for c, d in enumerate(ds):
        y = d * inv * g_ref[:, c * LANES:(c + 1) * LANES] + b_ref[:, c * LANES:(c + 1) * LANES]
        out.append((y * jax.nn.sigmoid(y)).astype(BF16))
    return out


def _conv_kernel(hist_ref, u_ref, w_ref, cb_ref, g_ref, b_ref, c_ref, ext_ref, tail_ref, y_ref, *, tm, taps, qb):
    ct = u_ref.shape[0]
    ch = ct * LANES
    nq = tm // SUBLANES
    nb = taps - 1
    hb = hist_ref.shape[2]
    t = pl.program_id(1)

    @pl.when(t == 0)
    def _():
        for c in range(ct):
            for e in range(nb):
                row = hist_ref[0, c, hb - nb + e:hb - nb + e + 1, :]
                tail_ref[c, e * SUBLANES:(e + 1) * SUBLANES, :] = jnp.broadcast_to(row, (SUBLANES, LANES))

    @pl.when(t > 0)
    def _():
        tail_ref[...] = ext_ref[:, nq * SUBLANES:(nq + nb) * SUBLANES, :]

    def load(q, carry):
        dst = pl.multiple_of((nb + q) * SUBLANES, SUBLANES)
        for c in range(ct):
            ext_ref[c, pl.ds(dst, SUBLANES), :] = u_ref[c, pl.ds(q, SUBLANES, stride=nq), :]
        return carry

    jax.lax.fori_loop(0, nq, load, 0)

    first_strip = jax.lax.broadcasted_iota(jnp.int32, (SUBLANES, LANES), 0) == 0
    for c in range(ct):
        for e in range(nb):
            cur = ext_ref[c, (nq + e) * SUBLANES:(nq + e + 1) * SUBLANES, :]
            prv = tail_ref[c, e * SUBLANES:(e + 1) * SUBLANES, :]
            ext_ref[c, e * SUBLANES:(e + 1) * SUBLANES, :] = jnp.where(
                first_strip, pltpu.roll(prv, 1, 0), pltpu.roll(cur, 1, 0))

    def conv_block(ib, carry):
        rows = qb * SUBLANES
        for c in range(ct):
            lanes = slice(c * LANES, (c + 1) * LANES)
            acc = w_ref[0:1, lanes] * ext_ref[c, pl.ds(pl.multiple_of(ib * rows, rows), rows), :]
            for j in range(1, taps):
                src = pl.multiple_of(ib * rows + j * SUBLANES, SUBLANES)
                acc = acc + w_ref[j:j + 1, lanes] * ext_ref[c, pl.ds(src, rows), :]
            y_ref[c, pl.ds(pl.multiple_of(ib * rows, rows), rows), :] = acc + cb_ref[:, lanes]
        return carry

    jax.lax.fori_loop(0, nq // qb, conv_block, 0)

    for r0 in range(0, tm, BF16_ROWS):
        tiles = []
        for c in range(ct):
            halves = []
            for rr in range(r0, r0 + BF16_ROWS, SUBLANES):
                s, q = divmod(rr, nq)
                halves.append(y_ref[c, pl.ds(q * SUBLANES + s, SUBLANES, stride=SUBLANES), :])
            tiles.append(jnp.concatenate(halves, axis=0))
        for c, o in enumerate(_swish_ln_tiles(tiles, g_ref, b_ref, ch)):
            c_ref[r0:r0 + BF16_ROWS, c * LANES:(c + 1) * LANES] = o


def _conv_module(hist, u, w, cb, g, b, l, *, tm, tiles_per_seg):
    ct, rows, _ = u.shape
    ch = ct * LANES
    n_seg, _, hb, _ = hist.shape
    taps = w.shape[1]
    nq = tm // SUBLANES
    qb = 8
    assert nq % qb == 0 and hb >= taps - 1 and nq >= taps - 1
    kern = functools.partial(_conv_kernel, tm=tm, taps=taps, qb=qb)
    return pl.pallas_call(
        kern,
        grid=(n_seg, tiles_per_seg),
        in_specs=[pl.BlockSpec((1, ct, hb, LANES), lambda s, t: (s, 0, 0, 0)),
                  pl.BlockSpec((ct, tm, LANES), lambda s, t: (0, s * tiles_per_seg + t, 0)),
                  _layer_spec(l, taps, ch), _layer_spec(l, 1, ch), _layer_spec(l, 1, ch), _layer_spec(l, 1, ch)],
        out_specs=pl.BlockSpec((tm, ch), lambda s, t: (s * tiles_per_seg + t, 0)),
        out_shape=jax.ShapeDtypeStruct((rows, ch), BF16),
        scratch_shapes=[pltpu.VMEM((ct, tm + (taps - 1) * SUBLANES, LANES), F32),
                        pltpu.VMEM((ct, (taps - 1) * SUBLANES, LANES), F32),
                        pltpu.VMEM((ct, tm, LANES), F32)],
        compiler_params=_params("arbitrary", "arbitrary"),
        name="conv_module",
    )(hist, u, w, cb, g, b)


def _conv_sample_kernel(hist_ref, u_ref, w_ref, cb_ref, g_ref, b_ref, c_ref):
    n_hist = hist_ref.shape[0]
    ct, t_steps = u_ref.shape[0], u_ref.shape[1]
    for t in range(t_steps):
        tiles = []
        for c in range(ct):
            lanes = slice(c * LANES, (c + 1) * LANES)
            acc = None
            for j in range(w_ref.shape[0]):
                s = t + j
                src = hist_ref[s, :, lanes] if s < n_hist else u_ref[c, s - n_hist]
                term = w_ref[j:j + 1, lanes] * src
                acc = term if acc is None else acc + term
            tiles.append(acc + cb_ref[:, lanes])
        for c, o in enumerate(_swish_ln_tiles(tiles, g_ref, b_ref, ct * LANES)):
            c_ref[t, :, c * LANES:(c + 1) * LANES] = o


def _conv_module_sample(hist, u, w, cb, g, b, l, *, bb):
    ct, t_steps, n_seq, _ = u.shape
    ch = ct * LANES
    n_hist = hist.shape[1]
    taps = w.shape[1]
    assert n_hist == taps - 1
    return pl.pallas_call(
        _conv_sample_kernel,
        grid=(n_seq // bb,),
        in_specs=[pl.BlockSpec((None, n_hist, bb, ch), lambda i: (l, 0, i, 0)),
                  pl.BlockSpec((ct, t_steps, bb, LANES), lambda i: (0, 0, i, 0)),
                  _layer_spec(l, taps, ch), _layer_spec(l, 1, ch), _layer_spec(l, 1, ch), _layer_spec(l, 1, ch)],
        out_specs=pl.BlockSpec((t_steps, bb, ch), lambda i: (0, i, 0)),
        out_shape=jax.ShapeDtypeStruct((t_steps, n_seq, ch), BF16),
        compiler_params=_params("arbitrary"),
        name="conv_module_sample",
    )(hist, u, w, cb, g, b)


def _attn_prompt_kernel(sink_ref, q_ref, kvp_ref, kvc_ref, o_ref, *, n_heads, group):
    blk = q_ref.shape[0]
    i = pl.program_id(1)
    kv = jnp.concatenate([kvp_ref[...], kvc_ref[...]], axis=0)
    kband = kv[:, 0:LANES]
    vband = kv[:, LANES:2 * LANES]
    kroll = pltpu.roll(kband, HEAD_DIM, 1)
    vroll = pltpu.roll(vband, HEAD_DIM, 1)
    lane = jax.lax.broadcasted_iota(jnp.int32, (2 * blk, LANES), 1)
    lo = lane < HEAD_DIM

    def lo_hi(x_lo, x_hi):
        return jnp.concatenate([jnp.where(lo, x_lo, 0.0), jnp.where(lo, 0.0, x_hi)], axis=0).astype(BF16)

    kab = [lo_hi(kband, kroll), lo_hi(kroll, kband)]
    vab = [lo_hi(vband, vroll), lo_hi(vroll, vband)]

    r = jax.lax.broadcasted_iota(jnp.int32, (blk, 2 * blk), 0)
    c = jax.lax.broadcasted_iota(jnp.int32, (blk, 2 * blk), 1)
    mask = (c >= r) & (c <= r + WINDOW) & ((c >= blk) | (i > 0))
    out_lo = jax.lax.broadcasted_iota(jnp.int32, (blk, LANES), 1) < HEAD_DIM

    pairs_per_kv = group // 2
    for kvh in range(n_heads // group):
        pairs = range(kvh * pairs_per_kv, (kvh + 1) * pairs_per_kv)
        q4 = jnp.concatenate([q_ref[:, p * LANES:(p + 1) * LANES] for p in pairs], axis=0)
        s4 = jax.lax.dot_general(q4, kab[kvh], (((1,), (1,)), ((), ())), preferred_element_type=F32)
        p_rows, inv_rows = [], []
        for n, pair in enumerate(pairs):
            ps, invs = [], []
            for half in range(2):
                sink = sink_ref[2 * pair + half]
                s = jnp.where(mask, s4[n * blk:(n + 1) * blk, half * 2 * blk:(half + 1) * 2 * blk], -jnp.inf)
                m = jnp.maximum(jnp.max(s, axis=-1, keepdims=True), sink)
                p = jnp.exp(s - m)
                denom = jnp.sum(p, axis=-1, keepdims=True) + jnp.exp(sink - m)
                ps.append(p.astype(BF16))
                invs.append(1.0 / denom)
            p_rows.append(jnp.concatenate(ps, axis=1))
            inv_rows.append(jnp.where(out_lo, invs[0], invs[1]))
        o4 = jnp.dot(jnp.concatenate(p_rows, axis=0), vab[kvh], preferred_element_type=F32)
        for n, pair in enumerate(pairs):
            o_ref[:, pair * LANES:(pair + 1) * LANES] = (o4[n * blk:(n + 1) * blk] * inv_rows[n]).astype(BF16)


def _attn_prompt(sinks, q, kv, *, n_seg, blocks_per_seg, blk):
    rows, attn_w = q.shape
    kv_w = kv.shape[1]
    n_heads = attn_w // HEAD_DIM
    kern = functools.partial(_attn_prompt_kernel, n_heads=n_heads, group=n_heads // N_KV_HEADS)
    return pl.pallas_call(
        kern,
        grid=(n_seg, blocks_per_seg),
        in_specs=[pl.BlockSpec(memory_space=pltpu.SMEM),
                  pl.BlockSpec((blk, attn_w), lambda s, i: (s * blocks_per_seg + i, 0)),
                  pl.BlockSpec((blk, kv_w), lambda s, i: (jnp.maximum(s * blocks_per_seg + i - 1, 0), 0)),
                  pl.BlockSpec((blk, kv_w), lambda s, i: (s * blocks_per_seg + i, 0))],
        out_specs=pl.BlockSpec((blk, attn_w), lambda s, i: (s * blocks_per_seg + i, 0)),
        out_shape=jax.ShapeDtypeStruct((rows, attn_w), BF16),
        compiler_params=_params("arbitrary", "arbitrary"),
        name="attn_prompt",
    )(sinks, q, kv, kv)


def _attn_sample_kernel(sink_ref, q_ref, hk_ref, hv_ref, nkv_ref, o_ref, *, t_steps, group):
    bb, nq, _ = q_ref.shape
    npad = nkv_ref.shape[1]
    q = q_ref[...]
    hk = hk_ref[...].astype(BF16)
    hv = hv_ref[...].astype(BF16)
    nk = nkv_ref[:, :, 0:LANES].astype(BF16)
    nv = nkv_ref[:, :, LANES:2 * LANES].astype(BF16)
    s_h = jnp.einsum('bqd,bkd->bqk', q, hk, preferred_element_type=F32)
    s_n = jnp.einsum('bqd,bkd->bqk', q, nk, preferred_element_type=F32)
    row = jax.lax.broadcasted_iota(jnp.int32, (1, nq, 1), 1)
    t = (row // group) % t_steps
    kvh = row // (group * t_steps)
    head = kvh * group + row % group
    sink = jnp.zeros((1, nq, 1), F32)
    for h in range(N_KV_HEADS * group):
        sink = jnp.where(head == h, sink_ref[h], sink)
    j_h = jax.lax.broadcasted_iota(jnp.int32, (1, nq, WINDOW), 2)
    j_n = jax.lax.broadcasted_iota(jnp.int32, (1, nq, npad), 2)
    s_h = jnp.where(j_h >= t, s_h, -jnp.inf)
    s_n = jnp.where(j_n <= t, s_n, -jnp.inf)
    m = jnp.maximum(jnp.maximum(jnp.max(s_h, axis=-1, keepdims=True), jnp.max(s_n, axis=-1, keepdims=True)), sink)
    p_h = jnp.exp(s_h - m)
    p_n = jnp.exp(s_n - m)
    denom = jnp.sum(p_h, axis=-1, keepdims=True) + jnp.sum(p_n, axis=-1, keepdims=True) + jnp.exp(sink - m)
    o = (jnp.einsum('bqk,bkd->bqd', p_h.astype(BF16), hv, preferred_element_type=F32)
         + jnp.einsum('bqk,bkd->bqd', p_n.astype(BF16), nv, preferred_element_type=F32))
    lane_kvh = jax.lax.broadcasted_iota(jnp.int32, (1, nq, LANES), 2) // HEAD_DIM
    o_ref[...] = jnp.where(lane_kvh == kvh, o / denom, 0.0)


def _attn_sample(sinks, qpad, hist_k, hist_v, new_kv, l, *, t_steps, bb):
    n_seq, nq, _ = qpad.shape
    npad = new_kv.shape[1]
    n_heads = sinks.shape[0]
    kern = functools.partial(_attn_sample_kernel, t_steps=t_steps, group=n_heads // N_KV_HEADS)
    return pl.pallas_call(
        kern,
        grid=(n_seq // bb,),
        in_specs=[pl.BlockSpec(memory_space=pltpu.SMEM),
                  pl.BlockSpec((bb, nq, LANES), lambda i: (i, 0, 0)),
                  pl.BlockSpec((None, bb, WINDOW, LANES), lambda i: (l, i, 0, 0)),
                  pl.BlockSpec((None, bb, WINDOW, LANES), lambda i: (l, i, 0, 0)),
                  pl.BlockSpec((bb, npad, 2 * LANES), lambda i: (i, 0, 0))],
        out_specs=pl.BlockSpec((bb, nq, LANES), lambda i: (i, 0, 0)),
        out_shape=jax.ShapeDtypeStruct((n_seq, nq, LANES), F32),
        compiler_params=_params("arbitrary"),
        name="attn_sample",
    )(sinks, qpad, hist_k, hist_v, new_kv)


def _out_proj_kernel(a_ref, c_ref, x_ref, w_ref, b_ref, g_ref, beta_ref, o_ref, *, alpha):
    aw = a_ref.shape[1]
    hm = a_ref.shape[0] // 2
    for h in range(2):
        rows = slice(h * hm, (h + 1) * hm)
        mix = (jnp.dot(a_ref[rows, :], w_ref[0:aw, :], preferred_element_type=F32)
               + jnp.dot(c_ref[rows, :], w_ref[aw:, :], preferred_element_type=F32) + b_ref[...])
        o_ref[rows, :] = _layer_norm(alpha * x_ref[rows, :] + mix, g_ref[...], beta_ref[...])


def _out_proj(attn, c, x, w, b, g, beta, l, *, tm, alpha):
    rows, d = x.shape
    aw, cw = attn.shape[1], c.shape[1]
    return pl.pallas_call(
        functools.partial(_out_proj_kernel, alpha=alpha),
        grid=(rows // tm,),
        in_specs=[pl.BlockSpec((tm, aw), lambda i: (i, 0)),
                  pl.BlockSpec((tm, cw), lambda i: (i, 0)),
                  pl.BlockSpec((tm, d), lambda i: (i, 0)),
                  pl.BlockSpec((None, aw + cw, d), lambda i: (l, 0, 0), pipeline_mode=pl.Buffered(1)),
                  _layer_spec(l, 1, d), _layer_spec(l, 1, d), _layer_spec(l, 1, d)],
        out_specs=pl.BlockSpec((tm, d), lambda i: (i, 0)),
        out_shape=jax.ShapeDtypeStruct((rows, d), F32),
        compiler_params=_params("arbitrary"),
        name="out_proj",
    )(attn, c, x, w, b, g, beta)


def _ffn_kernel(x_ref, hu_ref, hg_ref, wu_ref, wg_ref, cwu_ref, cwg_ref, cbu_ref, cbg_ref, wdn_ref, g_ref, beta_ref,
                o_ref, su_ref, sg_ref, xb_ref, hsu_ref, hsg_ref, p_ref, *carry,
                tm, prev, stride, tiles_per_seg, st0, st_rows, alpha, rb, n_split, dup_cols):
    i = pl.program_id(0)
    j = pl.program_id(1)
    nj = pl.num_programs(1)
    fc = p_ref.shape[1]
    hm = tm // n_split

    @pl.when(j == 0)
    def _():
        xb_ref[...] = x_ref[...].astype(BF16)
        o_ref[...] = jnp.zeros_like(o_ref)
        if carry:
            @pl.when(i == 0)
            def _():
                for cr in carry:
                    cr[...] = jnp.zeros_like(cr)

    for h in range(n_split):
        xs = xb_ref[h * hm:(h + 1) * hm, :]
        hsu_ref[prev + h * hm:prev + (h + 1) * hm, :] = jnp.dot(xs, wu_ref[0], preferred_element_type=F32)
        hsg_ref[prev + h * hm:prev + (h + 1) * hm, :] = jnp.dot(xs, wg_ref[0], preferred_element_type=F32)

    if carry:
        first = (i % tiles_per_seg) == 0
        for hs_ref, hist_ref, cr in ((hsu_ref, hu_ref, carry[0]), (hsg_ref, hg_ref, carry[1])):
            hs_ref[0:prev, :] = jnp.where(first, hist_ref[0], cr[j])
            cr[j] = hs_ref[tm:tm + prev, :]
    else:
        hsu_ref[0:prev, :] = hu_ref[0]
        hsg_ref[0:prev, :] = hg_ref[0]
    for st_ref, hs_ref in ((su_ref, hsu_ref), (sg_ref, hsg_ref)):
        v = hs_ref[prev + st0:prev + st0 + st_rows, :]
        if dup_cols:
            shifted = jnp.concatenate([v[:, dup_cols:], jnp.zeros((st_rows, dup_cols), F32)], axis=1)
            v = jnp.where(j == nj - 1, shifted, v)
        st_ref[0] = v

    def conv(hs_ref, cw_ref, cb_ref, r0):
        return (cw_ref[0, 2:3, :] * hs_ref[prev + r0:prev + r0 + rb, :]
                + cw_ref[0, 1:2, :] * hs_ref[prev - stride + r0:prev - stride + r0 + rb, :]
                + cw_ref[0, 0:1, :] * hs_ref[prev - 2 * stride + r0:prev - 2 * stride + r0 + rb, :]
                + cb_ref[0])

    col = jax.lax.broadcasted_iota(jnp.int32, (rb, fc), 1)
    keep = (col >= dup_cols) | (j < nj - 1)
    for h in range(n_split):
        for r0 in range(h * hm, (h + 1) * hm, rb):
            yu = conv(hsu_ref, cwu_ref, cbu_ref, r0)
            yg = conv(hsg_ref, cwg_ref, cbg_ref, r0)
            p_ref[r0:r0 + rb, :] = jnp.where(keep, yg * jax.nn.sigmoid(yg) * yu, 0.0).astype(BF16)
        rows = slice(h * hm, (h + 1) * hm)
        o_ref[rows, :] += jnp.dot(p_ref[rows, :], wdn_ref[0], preferred_element_type=F32)

    @pl.when(j == nj - 1)
    def _():
        o_ref[...] = _layer_norm(alpha * x_ref[...] + o_ref[...], g_ref[...], beta_ref[...])


def _conv_ffn(x, hist, wup, cw, cb, wdn, g, beta, l, *, tm, tiles_per_seg, stride, st0, st_rows, alpha, hist_base=0):
    rows, d = x.shape
    _, prev, _ = hist.shape
    d_ff = wdn.shape[1]
    fc = FF_CHUNK
    nblk = d_ff // LANES
    cblk = fc // LANES
    nj = pl.cdiv(d_ff, fc)
    dup_cols = nj * fc - d_ff
    n_tiles = rows // tm
    n_split = 2
    hm = tm // n_split
    rb = 32
    assert d_ff % LANES == 0 and tm % n_split == 0 and hm % rb == 0 and nblk >= cblk

    def blk0(j):
        return jnp.minimum(j * cblk, nblk - cblk)

    el = pl.Element
    col_u = lambda j: LANES * blk0(j)
    col_g = lambda j: LANES * (nblk + blk0(j))
    kern = functools.partial(_ffn_kernel, tm=tm, prev=prev, stride=stride, tiles_per_seg=tiles_per_seg,
                             st0=st0, st_rows=st_rows, alpha=alpha, rb=rb, n_split=n_split, dup_cols=dup_cols)
    scratch = [pltpu.VMEM((tm, d), BF16), pltpu.VMEM((prev + tm, fc), F32), pltpu.VMEM((prev + tm, fc), F32),
               pltpu.VMEM((tm, fc), BF16)]
    if tiles_per_seg > 1:
        scratch += [pltpu.VMEM((nj, prev, fc), F32), pltpu.VMEM((nj, prev, fc), F32)]
    hist_blk = (el(1), el(prev), el(fc))
    st_spec = pl.BlockSpec((1, st_rows, fc), lambda i, j: (i, 0, j))
    return pl.pallas_call(
        kern,
        grid=(n_tiles, nj),
        in_specs=[pl.BlockSpec((tm, d), lambda i, j: (i, 0)),
                  pl.BlockSpec(hist_blk, lambda i, j: (hist_base + i // tiles_per_seg, 0, col_u(j))),
                  pl.BlockSpec(hist_blk, lambda i, j: (hist_base + i // tiles_per_seg, 0, col_g(j))),
                  pl.BlockSpec((el(1), el(d), el(fc)), lambda i, j: (l, 0, col_u(j))),
                  pl.BlockSpec((el(1), el(d), el(fc)), lambda i, j: (l, 0, col_g(j))),
                  pl.BlockSpec((el(1), el(3), el(fc)), lambda i, j: (l, 0, col_u(j))),
                  pl.BlockSpec((el(1), el(3), el(fc)), lambda i, j: (l, 0, col_g(j))),
                  pl.BlockSpec((el(1), el(1), el(fc)), lambda i, j: (l, 0, col_u(j))),
                  pl.BlockSpec((el(1), el(1), el(fc)), lambda i, j: (l, 0, col_g(j))),
                  pl.BlockSpec((el(1), el(fc), el(d)), lambda i, j: (l, col_u(j), 0)),
                  _layer_spec(l, 1, d), _layer_spec(l, 1, d)],
        out_specs=[pl.BlockSpec((tm, d), lambda i, j: (i, 0)), st_spec, st_spec],
        out_shape=[jax.ShapeDtypeStruct((rows, d), F32),
                   jax.ShapeDtypeStruct((n_tiles, st_rows, nj * fc), F32),
                   jax.ShapeDtypeStruct((n_tiles, st_rows, nj * fc), F32)],
        scratch_shapes=scratch,
        compiler_params=_params("arbitrary", "arbitrary"),
        name="conv_ffn",
    )(x, hist, hist, wup, wup, cw, cw, cb, cb, wdn, g, beta)


def kernel(x_prompt, x_sample, state_attn_k, state_attn_v, state_conv, state_ffn_conv, meta_tokens, w_in, b_in, attn_sinks, conv_w, conv_b, conv_ln_g, conv_ln_b, w_out, b_out, ln1_g, ln1_b, ffn_w_up, ffn_conv_w, ffn_conv_b, ffn_w_down, ln2_g, ln2_b):
    batch, seq, d_model = x_prompt.shape
    dec_batch, dec_seq, _ = x_sample.shape
    depth = w_in.shape[0]
    conv_ch = conv_w.shape[2]
    conv_taps = conv_w.shape[1]
    ffn_taps = ffn_conv_w.shape[1]
    attn_w = d_model - conv_ch
    kv_w = 2 * N_KV_HEADS * HEAD_DIM
    n_heads = attn_w // HEAD_DIM
    group = n_heads // N_KV_HEADS
    d_ff = ffn_w_down.shape[1]
    ct = conv_ch // LANES
    alpha = (2 * depth) ** 0.25
    assert ffn_taps == 3 and kv_w == 2 * LANES and dec_batch % SUBLANES == 0
    assert ffn_taps - 1 <= dec_seq <= min(conv_taps - 1, SUBLANES) and seq >= WINDOW >= conv_taps

    seq_all = N_META + seq
    lp = _round_up(seq_all, WINDOW)
    tiles_p = 6
    tm_p = lp // tiles_p
    assert tm_p % 64 == 0
    meta = jnp.broadcast_to(meta_tokens[None].astype(x_prompt.dtype), (batch, N_META, d_model))
    xp = jnp.concatenate([meta, x_prompt, jnp.zeros((batch, lp - seq_all, d_model), x_prompt.dtype)], axis=1)
    xp = xp.reshape(batch * lp, d_model)
    rows_s = dec_seq * dec_batch
    xs = jnp.swapaxes(x_sample, 0, 1).reshape(rows_s, d_model)

    conv_hist_rows = 32
    ffn_prev_p = SUBLANES
    last_p = seq_all - 1
    ffn_state_tile = last_p // tm_p
    ffn_st0 = (last_p % tm_p) // SUBLANES * SUBLANES
    assert (last_p - 1) // tm_p == ffn_state_tile and (last_p - 1) % tm_p >= ffn_st0

    w_in_b, w_out_b = w_in.astype(BF16), w_out.astype(BF16)
    w_up_b, w_dn_b = ffn_w_up.astype(BF16), ffn_w_down.astype(BF16)
    rows3 = lambda v: v[:, None, :]
    b_in3, conv_b3, cg3, cbt3 = rows3(b_in), rows3(conv_b), rows3(conv_ln_g), rows3(conv_ln_b)
    b_out3, g1, be1, g2, be2 = rows3(b_out), rows3(ln1_g), rows3(ln1_b), rows3(ln2_g), rows3(ln2_b)
    fcb3 = rows3(ffn_conv_b)
    zero_conv_hist = jnp.zeros((batch, ct, conv_hist_rows, LANES), F32)
    zero_ffn_hist = jnp.zeros((batch, ffn_prev_p, 2 * d_ff), F32)
    hk_all = state_attn_k.reshape(depth, dec_batch, WINDOW, LANES)
    hv_all = state_attn_v.reshape(depth, dec_batch, WINDOW, LANES)
    conv_hist_all = jnp.swapaxes(state_conv, 1, 2)
    st_rows = (ffn_taps - 1) * dec_batch
    ffn_hist_all = jnp.swapaxes(state_ffn_conv, 1, 2).reshape(depth, st_rows, 2 * d_ff)

    pk, pv, pc, pf, sk, sv, sc, sf = [], [], [], [], [], [], [], []
    for l in range(depth):
        sinks = attn_sinks[l]

        q, kv, u = _in_proj(xp, w_in_b, b_in3, l, tm=tm_p, attn_w=attn_w, conv_ch=conv_ch, kv_w=kv_w)
        attn = _attn_prompt(sinks, q, kv, n_seg=batch, blocks_per_seg=lp // WINDOW, blk=WINDOW)
        c = _conv_module(zero_conv_hist, u, conv_w, conv_b3, cg3, cbt3, l, tm=tm_p, tiles_per_seg=tiles_p)
        x1 = _out_proj(attn, c, xp, w_out_b, b_out3, g1, be1, l, tm=tm_p, alpha=alpha)
        xp, su, sg = _conv_ffn(x1, zero_ffn_hist, w_up_b, ffn_conv_w, fcb3, w_dn_b, g2, be2, l, tm=tm_p,
                               tiles_per_seg=tiles_p, stride=1, st0=ffn_st0, st_rows=SUBLANES, alpha=alpha)
        kv3 = kv.reshape(batch, lp, kv_w)[:, seq_all - WINDOW:seq_all]
        pk.append(kv3[..., :kv_w // 2].reshape(batch, WINDOW, N_KV_HEADS, HEAD_DIM))
        pv.append(kv3[..., kv_w // 2:].reshape(batch, WINDOW, N_KV_HEADS, HEAD_DIM))
        u_tail = u.reshape(ct, batch, lp, LANES)[:, :, seq_all - (conv_taps - 1):seq_all]
        pc.append(u_tail.transpose(1, 2, 0, 3).reshape(batch, conv_taps - 1, conv_ch))
        off = last_p % tm_p - ffn_st0 - (ffn_taps - 2)
        hst = jnp.concatenate([su[..., :d_ff], sg[..., :d_ff]], axis=-1).reshape(batch, tiles_p, SUBLANES, 2 * d_ff)
        pf.append(hst[:, ffn_state_tile, off:off + ffn_taps - 1])

        q, kv, u = _in_proj(xs, w_in_b, b_in3, l, tm=rows_s, attn_w=attn_w, conv_ch=conv_ch, kv_w=kv_w)
        q5 = q.reshape(dec_seq, dec_batch, N_KV_HEADS, group, HEAD_DIM).transpose(1, 2, 0, 3, 4)
        q5 = q5.reshape(dec_batch, N_KV_HEADS, dec_seq * group, HEAD_DIM)
        zq = jnp.zeros_like(q5[:, 0])
        qpad = jnp.concatenate([jnp.concatenate([q5[:, 0], zq], axis=-1),
                                jnp.concatenate([zq, q5[:, 1]], axis=-1)], axis=1)
        kv_new = kv.reshape(dec_seq, dec_batch, kv_w).transpose(1, 0, 2)
        kv_new_pad = jnp.pad(kv_new, ((0, 0), (0, SUBLANES - dec_seq), (0, 0)))
        o = _attn_sample(sinks, qpad, hk_all, hv_all, kv_new_pad, l, t_steps=dec_seq, bb=8)
        o = (o[..., :HEAD_DIM] + o[..., HEAD_DIM:]).reshape(dec_batch, N_KV_HEADS, dec_seq, group, HEAD_DIM)
        attn = o.transpose(2, 0, 1, 3, 4).reshape(rows_s, attn_w).astype(BF16)
        u4 = u.reshape(ct, dec_seq, dec_batch, LANES)
        c = _conv_module_sample(conv_hist_all, u4, conv_w, conv_b3, cg3, cbt3, l, bb=32)
        c = c.reshape(rows_s, conv_ch)
        x1 = _out_proj(attn, c, xs, w_out_b, b_out3, g1, be1, l, tm=rows_s, alpha=alpha)
        xs, su, sg = _conv_ffn(x1, ffn_hist_all, w_up_b, ffn_conv_w, fcb3, w_dn_b, g2, be2, l, tm=rows_s,
                               tiles_per_seg=1, stride=dec_batch, st0=rows_s - st_rows, st_rows=st_rows,
                               alpha=alpha, hist_base=l)
        sk.append(kv_new[..., :kv_w // 2].reshape(dec_batch, dec_seq, N_KV_HEADS, HEAD_DIM))
        sv.append(kv_new[..., kv_w // 2:].reshape(dec_batch, dec_seq, N_KV_HEADS, HEAD_DIM))
        sc.append(u4.transpose(2, 1, 0, 3).reshape(dec_batch, dec_seq, conv_ch))
        sf.append(jnp.concatenate([su[..., :d_ff], sg[..., :d_ff]], axis=-1).reshape(ffn_taps - 1, dec_batch, 2 * d_ff))

    y_prompt = xp.reshape(batch, lp, d_model)[:, N_META:seq_all]
    y_sample = jnp.swapaxes(xs.reshape(dec_seq, dec_batch, d_model), 0, 1)
    sample_k = jnp.concatenate([state_attn_k[:, :, dec_seq:], jnp.stack(sk)], axis=2)
    sample_v = jnp.concatenate([state_attn_v[:, :, dec_seq:], jnp.stack(sv)], axis=2)
    sample_c = jnp.concatenate([state_conv[:, :, dec_seq:], jnp.stack(sc)], axis=2)
    sample_f = jnp.swapaxes(jnp.stack(sf), 1, 2)
    return (y_prompt, y_sample, jnp.stack(pk), jnp.stack(pv), jnp.stack(pc), jnp.stack(pf),
            sample_k, sample_v, sample_c, sample_f)
```

```python
import functools

import jax
import jax.numpy as jnp
from jax.experimental import pallas as pl
from jax.experimental.pallas import tpu as pltpu

F32 = jnp.float32
BF16 = jnp.bfloat16

N_META = 16
HEAD_DIM = 64
N_KV_HEADS = 2
WINDOW = 128
LN_EPS = 1e-5
LANES = 128
SUBLANES = 8
BF16_ROWS = 16
FF_CHUNK = 512
VMEM_LIMIT = 56 * 1024 * 1024


def _round_up(x, m):
    return (x + m - 1) // m * m


def _layer_norm(y, g, b):
    mu = jnp.mean(y, axis=-1, keepdims=True)
    d = y - mu
    var = jnp.mean(d * d, axis=-1, keepdims=True)
    return d * jax.lax.rsqrt(var + LN_EPS) * g + b


def _params(*sem):
    return pltpu.CompilerParams(dimension_semantics=sem, vmem_limit_bytes=VMEM_LIMIT)


def _layer_spec(l, *block):
    zeros = (0,) * len(block)
    return pl.BlockSpec((None,) + block, lambda *_: (l,) + zeros)


class _SideCast:
    def __init__(self, src, layer, chunk_rows, n_steps, step_of):
        _, rows, cols = src.shape
        n_chunks = pl.cdiv(rows, chunk_rows)
        assert n_chunks <= n_steps and chunk_rows % BF16_ROWS == 0
        self.src = src
        chunk = lambda *g: jnp.minimum(step_of(*g), n_chunks - 1)
        self.in_spec = pl.BlockSpec((None, chunk_rows, cols), lambda *g: (layer, chunk(*g), 0))
        self.out_spec = pl.BlockSpec((chunk_rows, cols), lambda *g: (chunk(*g), 0))
        self.out_shape = jax.ShapeDtypeStruct((rows, cols), BF16)


def _run_side_casts(side_in, side_out):
    for src_ref, dst_ref in zip(side_in, side_out):
        dst_ref[...] = src_ref[...].astype(BF16)


def _in_proj_kernel(x_ref, w_ref, b_ref, q_ref, kv_ref, u_ref, *, attn_w, conv_ch, kv_w, q_scale):
    xb = x_ref[...].astype(BF16)

    def proj(c0, n):
        return jnp.dot(xb, w_ref[:, c0:c0 + n], preferred_element_type=F32) + b_ref[:, c0:c0 + n]

    q_ref[...] = (proj(0, attn_w) * q_scale).astype(BF16)
    kv_ref[...] = proj(attn_w, kv_w)
    a = proj(attn_w + kv_w, conv_ch)
    g = proj(attn_w + kv_w + conv_ch, conv_ch)
    u = a * jax.nn.sigmoid(g)
    for c in range(conv_ch // LANES):
        u_ref[c] = u[:, c * LANES:(c + 1) * LANES]


def _in_proj(x, w, b, l, wl, *, tm, attn_w, conv_ch, kv_w):
    rows, d = x.shape
    n = w.shape[2]
    ct = conv_ch // LANES
    kern = functools.partial(_in_proj_kernel, attn_w=attn_w, conv_ch=conv_ch, kv_w=kv_w,
                             q_scale=HEAD_DIM ** -0.5)
    return pl.pallas_call(
        kern,
        grid=(rows // tm,),
        in_specs=[pl.BlockSpec((tm, d), lambda i: (i, 0)),
                  pl.BlockSpec((None, d, n), lambda i: (wl, 0, 0), pipeline_mode=pl.Buffered(1)),
                  _layer_spec(l, 1, n)],
        out_specs=[pl.BlockSpec((tm, attn_w), lambda i: (i, 0)),
                   pl.BlockSpec((tm, kv_w), lambda i: (i, 0)),
                   pl.BlockSpec((ct, tm, LANES), lambda i: (0, i, 0))],
        out_shape=[jax.ShapeDtypeStruct((rows, attn_w), BF16),
                   jax.ShapeDtypeStruct((rows, kv_w), F32),
                   jax.ShapeDtypeStruct((ct, rows, LANES), F32)],
        compiler_params=_params("arbitrary"),
        name="in_proj",
    )(x, w, b)


def _swish_ln_tiles(tiles, g_ref, b_ref, ch):
    total = tiles[0]
    for tl in tiles[1:]:
        total = total + tl
    mu = jnp.sum(total, axis=-1, keepdims=True) / ch
    ds = [tl - mu for tl in tiles]
    sq = ds[0] * ds[0]
    for d in ds[1:]:
        sq = sq + d * d
    inv = jax.lax.rsqrt(jnp.sum(sq, axis=-1, keepdims=True) / ch + LN_EPS)
    out = []
    for c, d in enumerate(ds):
        y = d * inv * g_ref[:, c * LANES:(c + 1) * LANES] + b_ref[:, c * LANES:(c + 1) * LANES]
        out.append((y * jax.nn.sigmoid(y)).astype(BF16))
    return out


def _conv_kernel(hist_ref, u_ref, w_ref, cb_ref, g_ref, b_ref, c_ref, ext_ref, tail_ref, y_ref, *, tm, taps, qb):
    ct = u_ref.shape[0]
    ch = ct * LANES
    nq = tm // SUBLANES
    nb = taps - 1
    hb = hist_ref.shape[2]
    t = pl.program_id(1)

    @pl.when(t == 0)
    def _():
        for c in range(ct):
            for e in range(nb):
                row = hist_ref[0, c, hb - nb + e:hb - nb + e + 1, :]
                tail_ref[c, e * SUBLANES:(e + 1) * SUBLANES, :] = jnp.broadcast_to(row, (SUBLANES, LANES))

    @pl.when(t > 0)
    def _():
        tail_ref[...] = ext_ref[:, nq * SUBLANES:(nq + nb) * SUBLANES, :]

    def load(q, carry):
        dst = pl.multiple_of((nb + q) * SUBLANES, SUBLANES)
        for c in range(ct):
            ext_ref[c, pl.ds(dst, SUBLANES), :] = u_ref[c, pl.ds(q, SUBLANES, stride=nq), :]
        return carry

    jax.lax.fori_loop(0, nq, load, 0)

    first_strip = jax.lax.broadcasted_iota(jnp.int32, (SUBLANES, LANES), 0) == 0
    for c in range(ct):
        for e in range(nb):
            cur = ext_ref[c, (nq + e) * SUBLANES:(nq + e + 1) * SUBLANES, :]
            prv = tail_ref[c, e * SUBLANES:(e + 1) * SUBLANES, :]
            ext_ref[c, e * SUBLANES:(e + 1) * SUBLANES, :] = jnp.where(
                first_strip, pltpu.roll(prv, 1, 0), pltpu.roll(cur, 1, 0))

    def conv_block(ib, carry):
        rows = qb * SUBLANES
        for c in range(ct):
            lanes = slice(c * LANES, (c + 1) * LANES)
            acc = w_ref[0:1, lanes] * ext_ref[c, pl.ds(pl.multiple_of(ib * rows, rows), rows), :]
            for j in range(1, taps):
                src = pl.multiple_of(ib * rows + j * SUBLANES, SUBLANES)
                acc = acc + w_ref[j:j + 1, lanes] * ext_ref[c, pl.ds(src, rows), :]
            y_ref[c, pl.ds(pl.multiple_of(ib * rows, rows), rows), :] = acc + cb_ref[:, lanes]
        return carry

    jax.lax.fori_loop(0, nq // qb, conv_block, 0)

    for r0 in range(0, tm, BF16_ROWS):
        tiles = []
        for c in range(ct):
            halves = []
            for rr in range(r0, r0 + BF16_ROWS, SUBLANES):
                s, q = divmod(rr, nq)
                halves.append(y_ref[c, pl.ds(q * SUBLANES + s, SUBLANES, stride=SUBLANES), :])
            tiles.append(jnp.concatenate(halves, axis=0))
        for c, o in enumerate(_swish_ln_tiles(tiles, g_ref, b_ref, ch)):
            c_ref[r0:r0 + BF16_ROWS, c * LANES:(c + 1) * LANES] = o


def _conv_module(hist, u, w, cb, g, b, l, *, tm, tiles_per_seg):
    ct, rows, _ = u.shape
    ch = ct * LANES
    n_seg, _, hb, _ = hist.shape
    taps = w.shape[1]
    nq = tm // SUBLANES
    qb = 8
    assert nq % qb == 0 and hb >= taps - 1 and nq >= taps - 1
    kern = functools.partial(_conv_kernel, tm=tm, taps=taps, qb=qb)
    return pl.pallas_call(
        kern,
        grid=(n_seg, tiles_per_seg),
        in_specs=[pl.BlockSpec((1, ct, hb, LANES), lambda s, t: (s, 0, 0, 0)),
                  pl.BlockSpec((ct, tm, LANES), lambda s, t: (0, s * tiles_per_seg + t, 0)),
                  _layer_spec(l, taps, ch), _layer_spec(l, 1, ch), _layer_spec(l, 1, ch), _layer_spec(l, 1, ch)],
        out_specs=pl.BlockSpec((tm, ch), lambda s, t: (s * tiles_per_seg + t, 0)),
        out_shape=jax.ShapeDtypeStruct((rows, ch), BF16),
        scratch_shapes=[pltpu.VMEM((ct, tm + (taps - 1) * SUBLANES, LANES), F32),
                        pltpu.VMEM((ct, (taps - 1) * SUBLANES, LANES), F32),
                        pltpu.VMEM((ct, tm, LANES), F32)],
        compiler_params=_params("arbitrary", "arbitrary"),
        name="conv_module",
    )(hist, u, w, cb, g, b)


def _conv_sample_kernel(hist_ref, u_ref, w_ref, cb_ref, g_ref, b_ref, c_ref):
    n_hist = hist_ref.shape[0]
    ct, t_steps = u_ref.shape[0], u_ref.shape[1]
    for t in range(t_steps):
        tiles = []
        for c in range(ct):
            lanes = slice(c * LANES, (c + 1) * LANES)
            acc = None
            for j in range(w_ref.shape[0]):
                s = t + j
                src = hist_ref[s, :, lanes] if s < n_hist else u_ref[c, s - n_hist]
                term = w_ref[j:j + 1, lanes] * src
                acc = term if acc is None else acc + term
            tiles.append(acc + cb_ref[:, lanes])
        for c, o in enumerate(_swish_ln_tiles(tiles, g_ref, b_ref, ct * LANES)):
            c_ref[t, :, c * LANES:(c + 1) * LANES] = o


def _conv_module_sample(hist, u, w, cb, g, b, l, *, bb):
    ct, t_steps, n_seq, _ = u.shape
    ch = ct * LANES
    n_hist = hist.shape[1]
    taps = w.shape[1]
    assert n_hist == taps - 1
    return pl.pallas_call(
        _conv_sample_kernel,
        grid=(n_seq // bb,),
        in_specs=[pl.BlockSpec((None, n_hist, bb, ch), lambda i: (l, 0, i, 0)),
                  pl.BlockSpec((ct, t_steps, bb, LANES), lambda i: (0, 0, i, 0)),
                  _layer_spec(l, taps, ch), _layer_spec(l, 1, ch), _layer_spec(l, 1, ch), _layer_spec(l, 1, ch)],
        out_specs=pl.BlockSpec((t_steps, bb, ch), lambda i: (0, i, 0)),
        out_shape=jax.ShapeDtypeStruct((t_steps, n_seq, ch), BF16),
        compiler_params=_params("arbitrary"),
        name="conv_module_sample",
    )(hist, u, w, cb, g, b)


def _attn_prompt_kernel(sink_ref, q_ref, kvp_ref, kvc_ref, *refs, n_heads, group):
    n_side = (len(refs) - 1) // 2
    o_ref = refs[n_side]
    _run_side_casts(refs[:n_side], refs[n_side + 1:])
    blk = q_ref.shape[0]
    i = pl.program_id(1)
    kv = jnp.concatenate([kvp_ref[...], kvc_ref[...]], axis=0)
    kband = kv[:, 0:LANES]
    vband = kv[:, LANES:2 * LANES]
    kroll = pltpu.roll(kband, HEAD_DIM, 1)
    vroll = pltpu.roll(vband, HEAD_DIM, 1)
    lane = jax.lax.broadcasted_iota(jnp.int32, (2 * blk, LANES), 1)
    lo = lane < HEAD_DIM

    def lo_hi(x_lo, x_hi):
        return jnp.concatenate([jnp.where(lo, x_lo, 0.0), jnp.where(lo, 0.0, x_hi)], axis=0).astype(BF16)

    kab = [lo_hi(kband, kroll), lo_hi(kroll, kband)]
    vab = [lo_hi(vband, vroll), lo_hi(vroll, vband)]

    r = jax.lax.broadcasted_iota(jnp.int32, (blk, 2 * blk), 0)
    c = jax.lax.broadcasted_iota(jnp.int32, (blk, 2 * blk), 1)
    mask = (c >= r) & (c <= r + WINDOW) & ((c >= blk) | (i > 0))
    out_lo = jax.lax.broadcasted_iota(jnp.int32, (blk, LANES), 1) < HEAD_DIM

    pairs_per_kv = group // 2
    for kvh in range(n_heads // group):
        pairs = range(kvh * pairs_per_kv, (kvh + 1) * pairs_per_kv)
        q4 = jnp.concatenate([q_ref[:, p * LANES:(p + 1) * LANES] for p in pairs], axis=0)
        s4 = jax.lax.dot_general(q4, kab[kvh], (((1,), (1,)), ((), ())), preferred_element_type=F32)
        p_rows, inv_rows = [], []
        for n, pair in enumerate(pairs):
            ps, invs = [], []
            for half in range(2):
                sink = sink_ref[2 * pair + half]
                s = jnp.where(mask, s4[n * blk:(n + 1) * blk, half * 2 * blk:(half + 1) * 2 * blk], -jnp.inf)
                m = jnp.maximum(jnp.max(s, axis=-1, keepdims=True), sink)
                p = jnp.exp(s - m)
                denom = jnp.sum(p, axis=-1, keepdims=True) + jnp.exp(sink - m)
                ps.append(p.astype(BF16))
                invs.append(1.0 / denom)
            p_rows.append(jnp.concatenate(ps, axis=1))
            inv_rows.append(jnp.where(out_lo, invs[0], invs[1]))
        o4 = jnp.dot(jnp.concatenate(p_rows, axis=0), vab[kvh], preferred_element_type=F32)
        for n, pair in enumerate(pairs):
            o_ref[:, pair * LANES:(pair + 1) * LANES] = (o4[n * blk:(n + 1) * blk] * inv_rows[n]).astype(BF16)


def _attn_prompt(sinks, q, kv, *, n_seg, blocks_per_seg, blk, side=()):
    rows, attn_w = q.shape
    casts = [_SideCast(w, wl, r, n_seg * blocks_per_seg, lambda s, i: s * blocks_per_seg + i) for w, wl, r in side]
    kv_w = kv.shape[1]
    n_heads = attn_w // HEAD_DIM
    kern = functools.partial(_attn_prompt_kernel, n_heads=n_heads, group=n_heads // N_KV_HEADS)
    return pl.pallas_call(
        kern,
        grid=(n_seg, blocks_per_seg),
        in_specs=[pl.BlockSpec(memory_space=pltpu.SMEM),
                  pl.BlockSpec((blk, attn_w), lambda s, i: (s * blocks_per_seg + i, 0)),
                  pl.BlockSpec((blk, kv_w), lambda s, i: (jnp.maximum(s * blocks_per_seg + i - 1, 0), 0)),
                  pl.BlockSpec((blk, kv_w), lambda s, i: (s * blocks_per_seg + i, 0))]
        + [c.in_spec for c in casts],
        out_specs=[pl.BlockSpec((blk, attn_w), lambda s, i: (s * blocks_per_seg + i, 0))]
        + [c.out_spec for c in casts],
        out_shape=[jax.ShapeDtypeStruct((rows, attn_w), BF16)] + [c.out_shape for c in casts],
        compiler_params=_params("arbitrary", "arbitrary"),
        name="attn_prompt",
    )(sinks, q, kv, kv, *[c.src for c in casts])


def _attn_sample_kernel(sink_ref, q_ref, hk_ref, hv_ref, nkv_ref, o_ref, *, t_steps, group):
    bb, nq, _ = q_ref.shape
    npad = nkv_ref.shape[1]
    q = q_ref[...]
    hk = hk_ref[...].astype(BF16)
    hv = hv_ref[...].astype(BF16)
    nk = nkv_ref[:, :, 0:LANES].astype(BF16)
    nv = nkv_ref[:, :, LANES:2 * LANES].astype(BF16)
    s_h = jnp.einsum('bqd,bkd->bqk', q, hk, preferred_element_type=F32)
    s_n = jnp.einsum('bqd,bkd->bqk', q, nk, preferred_element_type=F32)
    row = jax.lax.broadcasted_iota(jnp.int32, (1, nq, 1), 1)
    t = (row // group) % t_steps
    kvh = row // (group * t_steps)
    head = kvh * group + row % group
    sink = jnp.zeros((1, nq, 1), F32)
    for h in range(N_KV_HEADS * group):
        sink = jnp.where(head == h, sink_ref[h], sink)
    j_h = jax.lax.broadcasted_iota(jnp.int32, (1, nq, WINDOW), 2)
    j_n = jax.lax.broadcasted_iota(jnp.int32, (1, nq, npad), 2)
    s_h = jnp.where(j_h >= t, s_h, -jnp.inf)
    s_n = jnp.where(j_n <= t, s_n, -jnp.inf)
    m = jnp.maximum(jnp.maximum(jnp.max(s_h, axis=-1, keepdims=True), jnp.max(s_n, axis=-1, keepdims=True)), sink)
    p_h = jnp.exp(s_h - m)
    p_n = jnp.exp(s_n - m)
    denom = jnp.sum(p_h, axis=-1, keepdims=True) + jnp.sum(p_n, axis=-1, keepdims=True) + jnp.exp(sink - m)
    o = (jnp.einsum('bqk,bkd->bqd', p_h.astype(BF16), hv, preferred_element_type=F32)
         + jnp.einsum('bqk,bkd->bqd', p_n.astype(BF16), nv, preferred_element_type=F32))
    lane_kvh = jax.lax.broadcasted_iota(jnp.int32, (1, nq, LANES), 2) // HEAD_DIM
    o_ref[...] = jnp.where(lane_kvh == kvh, o / denom, 0.0)


def _attn_sample(sinks, qpad, hist_k, hist_v, new_kv, l, *, t_steps, bb):
    n_seq, nq, _ = qpad.shape
    npad = new_kv.shape[1]
    n_heads = sinks.shape[0]
    kern = functools.partial(_attn_sample_kernel, t_steps=t_steps, group=n_heads // N_KV_HEADS)
    return pl.pallas_call(
        kern,
        grid=(n_seq // bb,),
        in_specs=[pl.BlockSpec(memory_space=pltpu.SMEM),
                  pl.BlockSpec((bb, nq, LANES), lambda i: (i, 0, 0)),
                  pl.BlockSpec((None, bb, WINDOW, LANES), lambda i: (l, i, 0, 0)),
                  pl.BlockSpec((None, bb, WINDOW, LANES), lambda i: (l, i, 0, 0)),
                  pl.BlockSpec((bb, npad, 2 * LANES), lambda i: (i, 0, 0))],
        out_specs=pl.BlockSpec((bb, nq, LANES), lambda i: (i, 0, 0)),
        out_shape=jax.ShapeDtypeStruct((n_seq, nq, LANES), F32),
        compiler_params=_params("arbitrary"),
        name="attn_sample",
    )(sinks, qpad, hist_k, hist_v, new_kv)


def _out_proj_kernel(a_ref, c_ref, x_ref, w_ref, b_ref, g_ref, beta_ref, *refs, alpha):
    n_side = (len(refs) - 1) // 2
    o_ref = refs[n_side]
    _run_side_casts(refs[:n_side], refs[n_side + 1:])
    aw = a_ref.shape[1]
    hm = a_ref.shape[0] // 2
    for h in range(2):
        rows = slice(h * hm, (h + 1) * hm)
        mix = (jnp.dot(a_ref[rows, :], w_ref[0:aw, :], preferred_element_type=F32)
               + jnp.dot(c_ref[rows, :], w_ref[aw:, :], preferred_element_type=F32) + b_ref[...])
        o_ref[rows, :] = _layer_norm(alpha * x_ref[rows, :] + mix, g_ref[...], beta_ref[...])


def _out_proj(attn, c, x, w, b, g, beta, l, wl, *, tm, alpha, side=()):
    rows, d = x.shape
    aw, cw = attn.shape[1], c.shape[1]
    casts = [_SideCast(sw, wl, r, rows // tm, lambda i: i) for sw, wl, r in side]
    return pl.pallas_call(
        functools.partial(_out_proj_kernel, alpha=alpha),
        grid=(rows // tm,),
        in_specs=[pl.BlockSpec((tm, aw), lambda i: (i, 0)),
                  pl.BlockSpec((tm, cw), lambda i: (i, 0)),
                  pl.BlockSpec((tm, d), lambda i: (i, 0)),
                  pl.BlockSpec((None, aw + cw, d), lambda i: (wl, 0, 0), pipeline_mode=pl.Buffered(1)),
                  _layer_spec(l, 1, d), _layer_spec(l, 1, d), _layer_spec(l, 1, d)] + [c_.in_spec for c_ in casts],
        out_specs=[pl.BlockSpec((tm, d), lambda i: (i, 0))] + [c_.out_spec for c_ in casts],
        out_shape=[jax.ShapeDtypeStruct((rows, d), F32)] + [c_.out_shape for c_ in casts],
        compiler_params=_params("arbitrary"),
        name="out_proj",
    )(attn, c, x, w, b, g, beta, *[c_.src for c_ in casts])


def _ffn_kernel(x_ref, hu_ref, hg_ref, wu_ref, wg_ref, cwu_ref, cwg_ref, cbu_ref, cbg_ref, wdn_ref, g_ref, beta_ref,
                *refs, tm, prev, stride, tiles_per_seg, st0, st_rows, alpha, rb, n_split, dup_cols, n_side):
    side_in, refs = refs[:n_side], refs[n_side:]
    o_ref, su_ref, sg_ref = refs[:3]
    side_out, refs = refs[3:3 + n_side], refs[3 + n_side:]
    xb_ref, hsu_ref, hsg_ref, p_ref = refs[:4]
    carry = refs[4:]
    i = pl.program_id(0)
    j = pl.program_id(1)
    nj = pl.num_programs(1)
    fc = p_ref.shape[1]
    hm = tm // n_split

    @pl.when(j == 0)
    def _():
        xb_ref[...] = x_ref[...].astype(BF16)
        o_ref[...] = jnp.zeros_like(o_ref)
        if carry:
            @pl.when(i == 0)
            def _():
                for cr in carry:
                    cr[...] = jnp.zeros_like(cr)

    for h in range(n_split):
        xs = xb_ref[h * hm:(h + 1) * hm, :]
        hsu_ref[prev + h * hm:prev + (h + 1) * hm, :] = jnp.dot(xs, wu_ref[0], preferred_element_type=F32)
        hsg_ref[prev + h * hm:prev + (h + 1) * hm, :] = jnp.dot(xs, wg_ref[0], preferred_element_type=F32)
    _run_side_casts(side_in, side_out)

    if carry:
        first = (i % tiles_per_seg) == 0
        for hs_ref, hist_ref, cr in ((hsu_ref, hu_ref, carry[0]), (hsg_ref, hg_ref, carry[1])):
            hs_ref[0:prev, :] = jnp.where(first, hist_ref[0], cr[j])
            cr[j] = hs_ref[tm:tm + prev, :]
    else:
        hsu_ref[0:prev, :] = hu_ref[0]
        hsg_ref[0:prev, :] = hg_ref[0]
    for st_ref, hs_ref in ((su_ref, hsu_ref), (sg_ref, hsg_ref)):
        v = hs_ref[prev + st0:prev + st0 + st_rows, :]
        if dup_cols:
            shifted = jnp.concatenate([v[:, dup_cols:], jnp.zeros((st_rows, dup_cols), F32)], axis=1)
            v = jnp.where(j == nj - 1, shifted, v)
        st_ref[0] = v

    def conv(hs_ref, cw_ref, cb_ref, r0):
        return (cw_ref[0, 2:3, :] * hs_ref[prev + r0:prev + r0 + rb, :]
                + cw_ref[0, 1:2, :] * hs_ref[prev - stride + r0:prev - stride + r0 + rb, :]
                + cw_ref[0, 0:1, :] * hs_ref[prev - 2 * stride + r0:prev - 2 * stride + r0 + rb, :]
                + cb_ref[0])

    col = jax.lax.broadcasted_iota(jnp.int32, (rb, fc), 1)
    keep = (col >= dup_cols) | (j < nj - 1)
    for h in range(n_split):
        for r0 in range(h * hm, (h + 1) * hm, rb):
            yu = conv(hsu_ref, cwu_ref, cbu_ref, r0)
            yg = conv(hsg_ref, cwg_ref, cbg_ref, r0)
            p_ref[r0:r0 + rb, :] = jnp.where(keep, yg * jax.nn.sigmoid(yg) * yu, 0.0).astype(BF16)
        rows = slice(h * hm, (h + 1) * hm)
        o_ref[rows, :] += jnp.dot(p_ref[rows, :], wdn_ref[0], preferred_element_type=F32)

    @pl.when(j == nj - 1)
    def _():
        o_ref[...] = _layer_norm(alpha * x_ref[...] + o_ref[...], g_ref[...], beta_ref[...])


def _conv_ffn(x, hist, wup, cw, cb, wdn, g, beta, l, wl, *, tm, tiles_per_seg, stride, st0, st_rows, alpha, hist_base=0,
              side=()):
    rows, d = x.shape
    _, prev, _ = hist.shape
    d_ff = wdn.shape[1]
    fc = FF_CHUNK
    nblk = d_ff // LANES
    cblk = fc // LANES
    nj = pl.cdiv(d_ff, fc)
    dup_cols = nj * fc - d_ff
    n_tiles = rows // tm
    n_split = 2
    hm = tm // n_split
    rb = 32
    assert d_ff % LANES == 0 and tm % n_split == 0 and hm % rb == 0 and nblk >= cblk

    def blk0(j):
        return jnp.minimum(j * cblk, nblk - cblk)

    el = pl.Element
    col_u = lambda j: LANES * blk0(j)
    col_g = lambda j: LANES * (nblk + blk0(j))
    casts = [_SideCast(sw, wl, r, n_tiles * nj, lambda i, j: i * nj + j) for sw, wl, r in side]
    kern = functools.partial(_ffn_kernel, tm=tm, prev=prev, stride=stride, tiles_per_seg=tiles_per_seg, st0=st0,
                             st_rows=st_rows, alpha=alpha, rb=rb, n_split=n_split, dup_cols=dup_cols,
                             n_side=len(casts))
    scratch = [pltpu.VMEM((tm, d), BF16), pltpu.VMEM((prev + tm, fc), F32), pltpu.VMEM((prev + tm, fc), F32),
               pltpu.VMEM((tm, fc), BF16)]
    if tiles_per_seg > 1:
        scratch += [pltpu.VMEM((nj, prev, fc), F32), pltpu.VMEM((nj, prev, fc), F32)]
    hist_blk = (el(1), el(prev), el(fc))
    st_spec = pl.BlockSpec((1, st_rows, fc), lambda i, j: (i, 0, j))
    return pl.pallas_call(
        kern,
        grid=(n_tiles, nj),
        in_specs=[pl.BlockSpec((tm, d), lambda i, j: (i, 0)),
                  pl.BlockSpec(hist_blk, lambda i, j: (hist_base + i // tiles_per_seg, 0, col_u(j))),
                  pl.BlockSpec(hist_blk, lambda i, j: (hist_base + i // tiles_per_seg, 0, col_g(j))),
                  pl.BlockSpec((el(1), el(d), el(fc)), lambda i, j: (wl, 0, col_u(j))),
                  pl.BlockSpec((el(1), el(d), el(fc)), lambda i, j: (wl, 0, col_g(j))),
                  pl.BlockSpec((el(1), el(3), el(fc)), lambda i, j: (l, 0, col_u(j))),
                  pl.BlockSpec((el(1), el(3), el(fc)), lambda i, j: (l, 0, col_g(j))),
                  pl.BlockSpec((el(1), el(1), el(fc)), lambda i, j: (l, 0, col_u(j))),
                  pl.BlockSpec((el(1), el(1), el(fc)), lambda i, j: (l, 0, col_g(j))),
                  pl.BlockSpec((el(1), el(fc), el(d)), lambda i, j: (wl, col_u(j), 0)),
                  _layer_spec(l, 1, d), _layer_spec(l, 1, d)] + [c_.in_spec for c_ in casts],
        out_specs=[pl.BlockSpec((tm, d), lambda i, j: (i, 0)), st_spec, st_spec] + [c_.out_spec for c_ in casts],
        out_shape=[jax.ShapeDtypeStruct((rows, d), F32),
                   jax.ShapeDtypeStruct((n_tiles, st_rows, nj * fc), F32),
                   jax.ShapeDtypeStruct((n_tiles, st_rows, nj * fc), F32)] + [c_.out_shape for c_ in casts],
        scratch_shapes=scratch,
        compiler_params=_params("arbitrary", "arbitrary"),
        name="conv_ffn",
    )(x, hist, hist, wup, wup, cw, cw, cb, cb, wdn, g, beta, *[c_.src for c_ in casts])


def kernel(x_prompt, x_sample, state_attn_k, state_attn_v, state_conv, state_ffn_conv, meta_tokens, w_in, b_in, attn_sinks, conv_w, conv_b, conv_ln_g, conv_ln_b, w_out, b_out, ln1_g, ln1_b, ffn_w_up, ffn_conv_w, ffn_conv_b, ffn_w_down, ln2_g, ln2_b):
    batch, seq, d_model = x_prompt.shape
    dec_batch, dec_seq, _ = x_sample.shape
    depth = w_in.shape[0]
    conv_ch = conv_w.shape[2]
    conv_taps = conv_w.shape[1]
    ffn_taps = ffn_conv_w.shape[1]
    attn_w = d_model - conv_ch
    kv_w = 2 * N_KV_HEADS * HEAD_DIM
    n_heads = attn_w // HEAD_DIM
    group = n_heads // N_KV_HEADS
    d_ff = ffn_w_down.shape[1]
    ct = conv_ch // LANES
    alpha = (2 * depth) ** 0.25
    assert ffn_taps == 3 and kv_w == 2 * LANES and dec_batch % SUBLANES == 0
    assert ffn_taps - 1 <= dec_seq <= min(conv_taps - 1, SUBLANES) and seq >= WINDOW >= conv_taps

    seq_all = N_META + seq
    lp = _round_up(seq_all, WINDOW)
    tiles_p = 6
    tm_p = lp // tiles_p
    assert tm_p % 64 == 0
    meta = jnp.broadcast_to(meta_tokens[None].astype(x_prompt.dtype), (batch, N_META, d_model))
    xp = jnp.concatenate([meta, x_prompt, jnp.zeros((batch, lp - seq_all, d_model), x_prompt.dtype)], axis=1)
    xp = xp.reshape(batch * lp, d_model)
    rows_s = dec_seq * dec_batch
    xs = jnp.swapaxes(x_sample, 0, 1).reshape(rows_s, d_model)

    conv_hist_rows = 32
    ffn_prev_p = SUBLANES
    last_p = seq_all - 1
    ffn_state_tile = last_p // tm_p
    ffn_st0 = (last_p % tm_p) // SUBLANES * SUBLANES
    assert (last_p - 1) // tm_p == ffn_state_tile and (last_p - 1) % tm_p >= ffn_st0

    rows3 = lambda v: v[:, None, :]
    b_in3, conv_b3, cg3, cbt3 = rows3(b_in), rows3(conv_b), rows3(conv_ln_g), rows3(conv_ln_b)
    b_out3, g1, be1, g2, be2 = rows3(b_out), rows3(ln1_g), rows3(ln1_b), rows3(ln2_g), rows3(ln2_b)
    fcb3 = rows3(ffn_conv_b)
    zero_conv_hist = jnp.zeros((batch, ct, conv_hist_rows, LANES), F32)
    zero_ffn_hist = jnp.zeros((batch, ffn_prev_p, 2 * d_ff), F32)
    hk_all = state_attn_k.reshape(depth, dec_batch, WINDOW, LANES)
    hv_all = state_attn_v.reshape(depth, dec_batch, WINDOW, LANES)
    conv_hist_all = jnp.swapaxes(state_conv, 1, 2)
    st_rows = (ffn_taps - 1) * dec_batch
    ffn_hist_all = jnp.swapaxes(state_ffn_conv, 1, 2).reshape(depth, st_rows, 2 * d_ff)

    w_in_b = w_in[:1].astype(BF16)
    w_out_b = w_out[:1].astype(BF16)
    w_up_b = w_dn_b = None

    pk, pv, pc, pf, sk, sv, sc, sf = [], [], [], [], [], [], [], []
    for l in range(depth):
        sinks = attn_sinks[l]
        first = l == 0
        nxt = l + 1 < depth

        q, kv, u = _in_proj(xp, w_in_b, b_in3, l, 0, tm=tm_p, attn_w=attn_w, conv_ch=conv_ch, kv_w=kv_w)
        attn, *cast = _attn_prompt(sinks, q, kv, n_seg=batch, blocks_per_seg=lp // WINDOW, blk=WINDOW,
                                   side=[(ffn_w_up, 0, 32)] if first else [])
        if first:
            w_up_b = cast[0][None]
        c = _conv_module(zero_conv_hist, u, conv_w, conv_b3, cg3, cbt3, l, tm=tm_p, tiles_per_seg=tiles_p)
        x1, *cast = _out_proj(attn, c, xp, w_out_b, b_out3, g1, be1, l, 0, tm=tm_p, alpha=alpha,
                              side=[(ffn_w_down, 0, 512)] if first else [])
        if first:
            w_dn_b = cast[0][None]
        next_w = [(w_in, l + 1, 16), (w_out, l + 1, 16), (ffn_w_up, l + 1, 16), (ffn_w_down, l + 1, 48)] if nxt else []
        xp, su, sg, *cast = _conv_ffn(x1, zero_ffn_hist, w_up_b, ffn_conv_w, fcb3, w_dn_b, g2, be2, l, 0, tm=tm_p,
                                      tiles_per_seg=tiles_p, stride=1, st0=ffn_st0, st_rows=SUBLANES, alpha=alpha,
                                      side=next_w)
        kv3 = kv.reshape(batch, lp, kv_w)[:, seq_all - WINDOW:seq_all]
        pk.append(kv3[..., :kv_w // 2].reshape(batch, WINDOW, N_KV_HEADS, HEAD_DIM))
        pv.append(kv3[..., kv_w // 2:].reshape(batch, WINDOW, N_KV_HEADS, HEAD_DIM))
        u_tail = u.reshape(ct, batch, lp, LANES)[:, :, seq_all - (conv_taps - 1):seq_all]
        pc.append(u_tail.transpose(1, 2, 0, 3).reshape(batch, conv_taps - 1, conv_ch))
        off = last_p % tm_p - ffn_st0 - (ffn_taps - 2)
        hst = jnp.concatenate([su[..., :d_ff], sg[..., :d_ff]], axis=-1).reshape(batch, tiles_p, SUBLANES, 2 * d_ff)
        pf.append(hst[:, ffn_state_tile, off:off + ffn_taps - 1])

        q, kv, u = _in_proj(xs, w_in_b, b_in3, l, 0, tm=rows_s, attn_w=attn_w, conv_ch=conv_ch, kv_w=kv_w)
        q5 = q.reshape(dec_seq, dec_batch, N_KV_HEADS, group, HEAD_DIM).transpose(1, 2, 0, 3, 4)
        q5 = q5.reshape(dec_batch, N_KV_HEADS, dec_seq * group, HEAD_DIM)
        zq = jnp.zeros_like(q5[:, 0])
        qpad = jnp.concatenate([jnp.concatenate([q5[:, 0], zq], axis=-1),
                                jnp.concatenate([zq, q5[:, 1]], axis=-1)], axis=1)
        kv_new = kv.reshape(dec_seq, dec_batch, kv_w).transpose(1, 0, 2)
        kv_new_pad = jnp.pad(kv_new, ((0, 0), (0, SUBLANES - dec_seq), (0, 0)))
        o = _attn_sample(sinks, qpad, hk_all, hv_all, kv_new_pad, l, t_steps=dec_seq, bb=8)
        o = (o[..., :HEAD_DIM] + o[..., HEAD_DIM:]).reshape(dec_batch, N_KV_HEADS, dec_seq, group, HEAD_DIM)
        attn = o.transpose(2, 0, 1, 3, 4).reshape(rows_s, attn_w).astype(BF16)
        u4 = u.reshape(ct, dec_seq, dec_batch, LANES)
        c = _conv_module_sample(conv_hist_all, u4, conv_w, conv_b3, cg3, cbt3, l, bb=32)
        c = c.reshape(rows_s, conv_ch)
        x1, = _out_proj(attn, c, xs, w_out_b, b_out3, g1, be1, l, 0, tm=rows_s, alpha=alpha)
        xs, su, sg = _conv_ffn(x1, ffn_hist_all, w_up_b, ffn_conv_w, fcb3, w_dn_b, g2, be2, l, 0, tm=rows_s,
                               tiles_per_seg=1, stride=dec_batch, st0=rows_s - st_rows, st_rows=st_rows,
                               alpha=alpha, hist_base=l)
        sk.append(kv_new[..., :kv_w // 2])
        sv.append(kv_new[..., kv_w // 2:])
        sc.append(u4.transpose(2, 1, 0, 3).reshape(dec_batch, dec_seq, conv_ch))
        sf.append(jnp.concatenate([su[..., :d_ff], sg[..., :d_ff]], axis=-1).reshape(ffn_taps - 1, dec_batch, 2 * d_ff))
        if nxt:
            w_in_b, w_out_b, w_up_b, w_dn_b = (w[None] for w in cast)

    y_prompt = xp.reshape(batch, lp, d_model)[:, N_META:seq_all]
    y_sample = jnp.swapaxes(xs.reshape(dec_seq, dec_batch, d_model), 0, 1)
    kv_shape = (depth, dec_batch, WINDOW, N_KV_HEADS, HEAD_DIM)
    sample_k = jnp.concatenate([hk_all[:, :, dec_seq:], jnp.stack(sk)], axis=2).reshape(kv_shape)
    sample_v = jnp.concatenate([hv_all[:, :, dec_seq:], jnp.stack(sv)], axis=2).reshape(kv_shape)
    sample_c = jnp.concatenate([state_conv[:, :, dec_seq:], jnp.stack(sc)], axis=2)
    sample_f = jnp.swapaxes(jnp.stack(sf), 1, 2)
    return (y_prompt, y_sample, jnp.stack(pk), jnp.stack(pv), jnp.stack(pc), jnp.stack(pf),
            sample_k, sample_v, sample_c, sample_f)
```

```python
import functools

import jax
import jax.numpy as jnp
from jax.experimental import pallas as pl
from jax.experimental.pallas import tpu as pltpu

F32 = jnp.float32
BF16 = jnp.bfloat16

N_META = 16
HEAD_DIM = 64
N_KV_HEADS = 2
WINDOW = 128
LN_EPS = 1e-5
LANES = 128
SUBLANES = 8
BF16_ROWS = 16
FF_CHUNK = 512
VMEM_LIMIT = 56 * 1024 * 1024


def _round_up(x, m):
    return (x + m - 1) // m * m


def _layer_norm(y, g, b):
    mu = jnp.mean(y, axis=-1, keepdims=True)
    d = y - mu
    var = jnp.mean(d * d, axis=-1, keepdims=True)
    return d * jax.lax.rsqrt(var + LN_EPS) * g + b


def _params(*sem):
    return pltpu.CompilerParams(dimension_semantics=sem, vmem_limit_bytes=VMEM_LIMIT)


def _layer_spec(l, *block):
    zeros = (0,) * len(block)
    return pl.BlockSpec((None,) + block, lambda *_: (l,) + zeros)


class _SideCast:
    def __init__(self, src, layer, chunk_rows, n_steps, step_of):
        _, rows, cols = src.shape
        n_chunks = pl.cdiv(rows, chunk_rows)
        assert n_chunks <= n_steps and chunk_rows % BF16_ROWS == 0
        self.src = src
        chunk = lambda *g: jnp.minimum(step_of(*g), n_chunks - 1)
        self.in_spec = pl.BlockSpec((None, chunk_rows, cols), lambda *g: (layer, chunk(*g), 0))
        self.out_spec = pl.BlockSpec((chunk_rows, cols), lambda *g: (chunk(*g), 0))
        self.out_shape = jax.ShapeDtypeStruct((rows, cols), BF16)


def _run_side_casts(side_in, side_out):
    for src_ref, dst_ref in zip(side_in, side_out):
        dst_ref[...] = src_ref[...].astype(BF16)


def _in_proj_kernel(x_ref, w_ref, b_ref, q_ref, kv_ref, u_ref, *, attn_w, conv_ch, kv_w, q_scale):
    xb = x_ref[...].astype(BF16)

    def proj(c0, n):
        return jnp.dot(xb, w_ref[:, c0:c0 + n], preferred_element_type=F32) + b_ref[:, c0:c0 + n]

    q_ref[...] = (proj(0, attn_w) * q_scale).astype(BF16)
    kv_ref[...] = proj(attn_w, kv_w)
    a = proj(attn_w + kv_w, conv_ch)
    g = proj(attn_w + kv_w + conv_ch, conv_ch)
    u = a * jax.nn.sigmoid(g)
    for c in range(conv_ch // LANES):
        u_ref[c] = u[:, c * LANES:(c + 1) * LANES]


def _in_proj(x, w, b, l, wl, *, tm, attn_w, conv_ch, kv_w):
    rows, d = x.shape
    n = w.shape[2]
    ct = conv_ch // LANES
    kern = functools.partial(_in_proj_kernel, attn_w=attn_w, conv_ch=conv_ch, kv_w=kv_w,
                             q_scale=HEAD_DIM ** -0.5)
    return pl.pallas_call(
        kern,
        grid=(rows // tm,),
        in_specs=[pl.BlockSpec((tm, d), lambda i: (i, 0)),
                  pl.BlockSpec((None, d, n), lambda i: (wl, 0, 0), pipeline_mode=pl.Buffered(1)),
                  _layer_spec(l, 1, n)],
        out_specs=[pl.BlockSpec((tm, attn_w), lambda i: (i, 0)),
                   pl.BlockSpec((tm, kv_w), lambda i: (i, 0)),
                   pl.BlockSpec((ct, tm, LANES), lambda i: (0, i, 0))],
        out_shape=[jax.ShapeDtypeStruct((rows, attn_w), BF16),
                   jax.ShapeDtypeStruct((rows, kv_w), F32),
                   jax.ShapeDtypeStruct((ct, rows, LANES), F32)],
        compiler_params=_params("arbitrary"),
        name="in_proj",
    )(x, w, b)


def _swish_ln_tiles(tiles, g_ref, b_ref, ch):
    total = tiles[0]
    for tl in tiles[1:]:
        total = total + tl
    mu = jnp.sum(total, axis=-1, keepdims=True) / ch
    ds = [tl - mu for tl in tiles]
    sq = ds[0] * ds[0]
    for d in ds[1:]:
        sq = sq + d * d
    inv = jax.lax.rsqrt(jnp.sum(sq, axis=-1, keepdims=True) / ch + LN_EPS)
    out = []
    for c, d in enumerate(ds):
        y = d * inv * g_ref[:, c * LANES:(c + 1) * LANES] + b_ref[:, c * LANES:(c + 1) * LANES]
        out.append((y * jax.nn.sigmoid(y)).astype(BF16))
    return out


def _conv_kernel(hist_ref, u_ref, w_ref, cb_ref, g_ref, b_ref, *refs, tm, taps, qb, n_side):
    c_ref = refs[n_side]
    ext_ref, tail_ref, y_ref = refs[2 * n_side + 1:]
    _run_side_casts(refs[:n_side], refs[n_side + 1:2 * n_side + 1])
    ct = u_ref.shape[0]
    ch = ct * LANES
    nq = tm // SUBLANES
    nb = taps - 1
    hb = hist_ref.shape[2]
    t = pl.program_id(1)

    @pl.when(t == 0)
    def _():
        for c in range(ct):
            for e in range(nb):
                row = hist_ref[0, c, hb - nb + e:hb - nb + e + 1, :]
                tail_ref[c, e * SUBLANES:(e + 1) * SUBLANES, :] = jnp.broadcast_to(row, (SUBLANES, LANES))

    @pl.when(t > 0)
    def _():
        tail_ref[...] = ext_ref[:, nq * SUBLANES:(nq + nb) * SUBLANES, :]

    def load(q, carry):
        dst = pl.multiple_of((nb + q) * SUBLANES, SUBLANES)
        for c in range(ct):
            ext_ref[c, pl.ds(dst, SUBLANES), :] = u_ref[c, pl.ds(q, SUBLANES, stride=nq), :]
        return carry

    jax.lax.fori_loop(0, nq, load, 0)

    first_strip = jax.lax.broadcasted_iota(jnp.int32, (SUBLANES, LANES), 0) == 0
    for c in range(ct):
        for e in range(nb):
            cur = ext_ref[c, (nq + e) * SUBLANES:(nq + e + 1) * SUBLANES, :]
            prv = tail_ref[c, e * SUBLANES:(e + 1) * SUBLANES, :]
            ext_ref[c, e * SUBLANES:(e + 1) * SUBLANES, :] = jnp.where(
                first_strip, pltpu.roll(prv, 1, 0), pltpu.roll(cur, 1, 0))

    def conv_block(ib, carry):
        rows = qb * SUBLANES
        for c in range(ct):
            lanes = slice(c * LANES, (c + 1) * LANES)
            acc = w_ref[0:1, lanes] * ext_ref[c, pl.ds(pl.multiple_of(ib * rows, rows), rows), :]
            for j in range(1, taps):
                src = pl.multiple_of(ib * rows + j * SUBLANES, SUBLANES)
                acc = acc + w_ref[j:j + 1, lanes] * ext_ref[c, pl.ds(src, rows), :]
            y_ref[c, pl.ds(pl.multiple_of(ib * rows, rows), rows), :] = acc + cb_ref[:, lanes]
        return carry

    jax.lax.fori_loop(0, nq // qb, conv_block, 0)

    for r0 in range(0, tm, BF16_ROWS):
        tiles = []
        for c in range(ct):
            halves = []
            for rr in range(r0, r0 + BF16_ROWS, SUBLANES):
                s, q = divmod(rr, nq)
                halves.append(y_ref[c, pl.ds(q * SUBLANES + s, SUBLANES, stride=SUBLANES), :])
            tiles.append(jnp.concatenate(halves, axis=0))
        for c, o in enumerate(_swish_ln_tiles(tiles, g_ref, b_ref, ch)):
            c_ref[r0:r0 + BF16_ROWS, c * LANES:(c + 1) * LANES] = o


def _conv_module(hist, u, w, cb, g, b, l, *, tm, tiles_per_seg, side=()):
    ct, rows, _ = u.shape
    ch = ct * LANES
    n_seg, _, hb, _ = hist.shape
    taps = w.shape[1]
    nq = tm // SUBLANES
    qb = 8
    assert nq % qb == 0 and hb >= taps - 1 and nq >= taps - 1
    casts = [_SideCast(sw, sl, r, n_seg * tiles_per_seg, lambda s, t: s * tiles_per_seg + t) for sw, sl, r in side]
    kern = functools.partial(_conv_kernel, tm=tm, taps=taps, qb=qb, n_side=len(casts))
    return pl.pallas_call(
        kern,
        grid=(n_seg, tiles_per_seg),
        in_specs=[pl.BlockSpec((1, ct, hb, LANES), lambda s, t: (s, 0, 0, 0)),
                  pl.BlockSpec((ct, tm, LANES), lambda s, t: (0, s * tiles_per_seg + t, 0)),
                  _layer_spec(l, taps, ch), _layer_spec(l, 1, ch), _layer_spec(l, 1, ch), _layer_spec(l, 1, ch)]
        + [c_.in_spec for c_ in casts],
        out_specs=[pl.BlockSpec((tm, ch), lambda s, t: (s * tiles_per_seg + t, 0))] + [c_.out_spec for c_ in casts],
        out_shape=[jax.ShapeDtypeStruct((rows, ch), BF16)] + [c_.out_shape for c_ in casts],
        scratch_shapes=[pltpu.VMEM((ct, tm + (taps - 1) * SUBLANES, LANES), F32),
                        pltpu.VMEM((ct, (taps - 1) * SUBLANES, LANES), F32),
                        pltpu.VMEM((ct, tm, LANES), F32)],
        compiler_params=_params("arbitrary", "arbitrary"),
        name="conv_module",
    )(hist, u, w, cb, g, b, *[c_.src for c_ in casts])


def _conv_sample_kernel(hist_ref, u_ref, w_ref, cb_ref, g_ref, b_ref, *refs):
    c_ref, ns_ref = refs[-2:]
    n_hist = hist_ref.shape[0]
    ct, t_steps = u_ref.shape[0], u_ref.shape[1]
    for k in range(ns_ref.shape[0]):
        for s in range(n_hist):
            src = s + t_steps
            if src < n_hist:
                ns_ref[k, :, s, :] = hist_ref[src]
            else:
                for c in range(ct):
                    ns_ref[k, :, s, c * LANES:(c + 1) * LANES] = u_ref[c, src - n_hist]
    for t in range(t_steps):
        tiles = []
        for c in range(ct):
            lanes = slice(c * LANES, (c + 1) * LANES)
            acc = None
            for j in range(w_ref.shape[0]):
                s = t + j
                src = hist_ref[s, :, lanes] if s < n_hist else u_ref[c, s - n_hist]
                term = w_ref[j:j + 1, lanes] * src
                acc = term if acc is None else acc + term
            tiles.append(acc + cb_ref[:, lanes])
        for c, o in enumerate(_swish_ln_tiles(tiles, g_ref, b_ref, ct * LANES)):
            c_ref[t, :, c * LANES:(c + 1) * LANES] = o


def _state_out(bufs, l, depth, block, n_inputs, first_out):
    tail = (0,) * (len(block) - 1)
    if bufs is None:
        return pl.BlockSpec((depth,) + block, lambda i: (0, i) + tail), [], [], {}
    specs = [pl.BlockSpec(memory_space=pl.ANY)] * len(bufs)
    aliases = {n_inputs + k: first_out + k for k in range(len(bufs))}
    return pl.BlockSpec((1,) + block, lambda i: (l, i) + tail), list(bufs), specs, aliases


def _conv_module_sample(hist, u, w, cb, g, b, l, state_buf, *, bb):
    depth, n_hist, n_seq, ch = hist.shape
    ct, t_steps, _, _ = u.shape
    taps = w.shape[1]
    assert n_hist == taps - 1 and ch == ct * LANES
    st_spec, bufs, buf_specs, aliases = _state_out(None if state_buf is None else [state_buf], l, depth,
                                                   (bb, n_hist, ch), 6, 1)
    return pl.pallas_call(
        _conv_sample_kernel,
        grid=(n_seq // bb,),
        in_specs=[pl.BlockSpec((None, n_hist, bb, ch), lambda i: (l, 0, i, 0)),
                  pl.BlockSpec((ct, t_steps, bb, LANES), lambda i: (0, 0, i, 0)),
                  _layer_spec(l, taps, ch), _layer_spec(l, 1, ch), _layer_spec(l, 1, ch), _layer_spec(l, 1, ch)]
        + buf_specs,
        out_specs=[pl.BlockSpec((t_steps, bb, ch), lambda i: (0, i, 0)), st_spec],
        out_shape=[jax.ShapeDtypeStruct((t_steps, n_seq, ch), BF16),
                   jax.ShapeDtypeStruct((depth, n_seq, n_hist, ch), F32)],
        input_output_aliases=aliases,
        compiler_params=_params("arbitrary"),
        name="conv_module_sample",
    )(hist, u, w, cb, g, b, *bufs)


def _attn_prompt_kernel(sink_ref, q_ref, kvp_ref, kvc_ref, *refs, n_heads, group):
    n_side = (len(refs) - 1) // 2
    o_ref = refs[n_side]
    _run_side_casts(refs[:n_side], refs[n_side + 1:])
    blk = q_ref.shape[0]
    i = pl.program_id(1)
    kv = jnp.concatenate([kvp_ref[...], kvc_ref[...]], axis=0)
    kband = kv[:, 0:LANES]
    vband = kv[:, LANES:2 * LANES]
    kroll = pltpu.roll(kband, HEAD_DIM, 1)
    vroll = pltpu.roll(vband, HEAD_DIM, 1)
    lane = jax.lax.broadcasted_iota(jnp.int32, (2 * blk, LANES), 1)
    lo = lane < HEAD_DIM

    def lo_hi(x_lo, x_hi):
        return jnp.concatenate([jnp.where(lo, x_lo, 0.0), jnp.where(lo, 0.0, x_hi)], axis=0).astype(BF16)

    kab = [lo_hi(kband, kroll), lo_hi(kroll, kband)]
    vab = [lo_hi(vband, vroll), lo_hi(vroll, vband)]

    r = jax.lax.broadcasted_iota(jnp.int32, (blk, 2 * blk), 0)
    c = jax.lax.broadcasted_iota(jnp.int32, (blk, 2 * blk), 1)
    mask = (c >= r) & (c <= r + WINDOW) & ((c >= blk) | (i > 0))
    out_lo = jax.lax.broadcasted_iota(jnp.int32, (blk, LANES), 1) < HEAD_DIM

    pairs_per_kv = group // 2
    for kvh in range(n_heads // group):
        pairs = range(kvh * pairs_per_kv, (kvh + 1) * pairs_per_kv)
        q4 = jnp.concatenate([q_ref[:, p * LANES:(p + 1) * LANES] for p in pairs], axis=0)
        s4 = jax.lax.dot_general(q4, kab[kvh], (((1,), (1,)), ((), ())), preferred_element_type=F32)
        p_rows, inv_rows = [], []
        for n, pair in enumerate(pairs):
            ps, invs = [], []
            for half in range(2):
                sink = sink_ref[2 * pair + half]
                s = jnp.where(mask, s4[n * blk:(n + 1) * blk, half * 2 * blk:(half + 1) * 2 * blk], -jnp.inf)
                m = jnp.maximum(jnp.max(s, axis=-1, keepdims=True), sink)
                p = jnp.exp(s - m)
                denom = jnp.sum(p, axis=-1, keepdims=True) + jnp.exp(sink - m)
                ps.append(p.astype(BF16))
                invs.append(1.0 / denom)
            p_rows.append(jnp.concatenate(ps, axis=1))
            inv_rows.append(jnp.where(out_lo, invs[0], invs[1]))
        o4 = jnp.dot(jnp.concatenate(p_rows, axis=0), vab[kvh], preferred_element_type=F32)
        for n, pair in enumerate(pairs):
            o_ref[:, pair * LANES:(pair + 1) * LANES] = (o4[n * blk:(n + 1) * blk] * inv_rows[n]).astype(BF16)


def _attn_prompt(sinks, q, kv, *, n_seg, blocks_per_seg, blk, side=()):
    rows, attn_w = q.shape
    casts = [_SideCast(w, wl, r, n_seg * blocks_per_seg, lambda s, i: s * blocks_per_seg + i) for w, wl, r in side]
    kv_w = kv.shape[1]
    n_heads = attn_w // HEAD_DIM
    kern = functools.partial(_attn_prompt_kernel, n_heads=n_heads, group=n_heads // N_KV_HEADS)
    return pl.pallas_call(
        kern,
        grid=(n_seg, blocks_per_seg),
        in_specs=[pl.BlockSpec(memory_space=pltpu.SMEM),
                  pl.BlockSpec((blk, attn_w), lambda s, i: (s * blocks_per_seg + i, 0)),
                  pl.BlockSpec((blk, kv_w), lambda s, i: (jnp.maximum(s * blocks_per_seg + i - 1, 0), 0)),
                  pl.BlockSpec((blk, kv_w), lambda s, i: (s * blocks_per_seg + i, 0))]
        + [c.in_spec for c in casts],
        out_specs=[pl.BlockSpec((blk, attn_w), lambda s, i: (s * blocks_per_seg + i, 0))]
        + [c.out_spec for c in casts],
        out_shape=[jax.ShapeDtypeStruct((rows, attn_w), BF16)] + [c.out_shape for c in casts],
        compiler_params=_params("arbitrary", "arbitrary"),
        name="attn_prompt",
    )(sinks, q, kv, kv, *[c.src for c in casts])


def _attn_sample_kernel(sink_ref, q_ref, hk_ref, hv_ref, nkv_ref, *refs, t_steps, group):
    o_ref, nk_ref, nv_ref = refs[-3:]
    bb, nq, _ = q_ref.shape
    npad = nkv_ref.shape[1]
    for st_ref, h_ref, lane0 in ((nk_ref, hk_ref, 0), (nv_ref, hv_ref, LANES)):
        for k in range(st_ref.shape[0]):
            st_ref[k, :, 0:WINDOW - t_steps, :] = h_ref[:, t_steps:WINDOW, :]
            st_ref[k, :, WINDOW - t_steps:WINDOW, :] = nkv_ref[:, 0:t_steps, lane0:lane0 + LANES]
    q = q_ref[...]
    hk = hk_ref[...].astype(BF16)
    hv = hv_ref[...].astype(BF16)
    nk = nkv_ref[:, :, 0:LANES].astype(BF16)
    nv = nkv_ref[:, :, LANES:2 * LANES].astype(BF16)
    s_h = jnp.einsum('bqd,bkd->bqk', q, hk, preferred_element_type=F32)
    s_n = jnp.einsum('bqd,bkd->bqk', q, nk, preferred_element_type=F32)
    row = jax.lax.broadcasted_iota(jnp.int32, (1, nq, 1), 1)
    t = (row // group) % t_steps
    kvh = row // (group * t_steps)
    head = kvh * group + row % group
    sink = jnp.zeros((1, nq, 1), F32)
    for h in range(N_KV_HEADS * group):
        sink = jnp.where(head == h, sink_ref[h], sink)
    j_h = jax.lax.broadcasted_iota(jnp.int32, (1, nq, WINDOW), 2)
    j_n = jax.lax.broadcasted_iota(jnp.int32, (1, nq, npad), 2)
    s_h = jnp.where(j_h >= t, s_h, -jnp.inf)
    s_n = jnp.where(j_n <= t, s_n, -jnp.inf)
    m = jnp.maximum(jnp.maximum(jnp.max(s_h, axis=-1, keepdims=True), jnp.max(s_n, axis=-1, keepdims=True)), sink)
    p_h = jnp.exp(s_h - m)
    p_n = jnp.exp(s_n - m)
    denom = jnp.sum(p_h, axis=-1, keepdims=True) + jnp.sum(p_n, axis=-1, keepdims=True) + jnp.exp(sink - m)
    o = (jnp.einsum('bqk,bkd->bqd', p_h.astype(BF16), hv, preferred_element_type=F32)
         + jnp.einsum('bqk,bkd->bqd', p_n.astype(BF16), nv, preferred_element_type=F32))
    lane_kvh = jax.lax.broadcasted_iota(jnp.int32, (1, nq, LANES), 2) // HEAD_DIM
    o_ref[...] = jnp.where(lane_kvh == kvh, o / denom, 0.0)


def _attn_sample(sinks, qpad, hist_k, hist_v, new_kv, l, state_bufs, *, t_steps, bb):
    n_seq, nq, _ = qpad.shape
    npad = new_kv.shape[1]
    n_heads = sinks.shape[0]
    kern = functools.partial(_attn_sample_kernel, t_steps=t_steps, group=n_heads // N_KV_HEADS)
    st_spec, bufs, buf_specs, aliases = _state_out(state_bufs, l, hist_k.shape[0], (bb, WINDOW, LANES), 5, 1)
    st_shape = jax.ShapeDtypeStruct(hist_k.shape, F32)
    return pl.pallas_call(
        kern,
        grid=(n_seq // bb,),
        in_specs=[pl.BlockSpec(memory_space=pltpu.SMEM),
                  pl.BlockSpec((bb, nq, LANES), lambda i: (i, 0, 0)),
                  pl.BlockSpec((None, bb, WINDOW, LANES), lambda i: (l, i, 0, 0)),
                  pl.BlockSpec((None, bb, WINDOW, LANES), lambda i: (l, i, 0, 0)),
                  pl.BlockSpec((bb, npad, 2 * LANES), lambda i: (i, 0, 0))] + buf_specs,
        out_specs=[pl.BlockSpec((bb, nq, LANES), lambda i: (i, 0, 0)), st_spec, st_spec],
        out_shape=[jax.ShapeDtypeStruct((n_seq, nq, LANES), F32), st_shape, st_shape],
        input_output_aliases=aliases,
        compiler_params=_params("arbitrary"),
        name="attn_sample",
    )(sinks, qpad, hist_k, hist_v, new_kv, *bufs)


def _out_proj_kernel(a_ref, c_ref, x_ref, w_ref, b_ref, g_ref, beta_ref, *refs, alpha):
    n_side = (len(refs) - 1) // 2
    o_ref = refs[n_side]
    _run_side_casts(refs[:n_side], refs[n_side + 1:])
    aw = a_ref.shape[1]
    hm = a_ref.shape[0] // 2
    for h in range(2):
        rows = slice(h * hm, (h + 1) * hm)
        mix = (jnp.dot(a_ref[rows, :], w_ref[0:aw, :], preferred_element_type=F32)
               + jnp.dot(c_ref[rows, :], w_ref[aw:, :], preferred_element_type=F32) + b_ref[...])
        o_ref[rows, :] = _layer_norm(alpha * x_ref[rows, :] + mix, g_ref[...], beta_ref[...])


def _out_proj(attn, c, x, w, b, g, beta, l, wl, *, tm, alpha, side=()):
    rows, d = x.shape
    aw, cw = attn.shape[1], c.shape[1]
    casts = [_SideCast(sw, wl, r, rows // tm, lambda i: i) for sw, wl, r in side]
    return pl.pallas_call(
        functools.partial(_out_proj_kernel, alpha=alpha),
        grid=(rows // tm,),
        in_specs=[pl.BlockSpec((tm, aw), lambda i: (i, 0)),
                  pl.BlockSpec((tm, cw), lambda i: (i, 0)),
                  pl.BlockSpec((tm, d), lambda i: (i, 0)),
                  pl.BlockSpec((None, aw + cw, d), lambda i: (wl, 0, 0), pipeline_mode=pl.Buffered(1)),
                  _layer_spec(l, 1, d), _layer_spec(l, 1, d), _layer_spec(l, 1, d)] + [c_.in_spec for c_ in casts],
        out_specs=[pl.BlockSpec((tm, d), lambda i: (i, 0))] + [c_.out_spec for c_ in casts],
        out_shape=[jax.ShapeDtypeStruct((rows, d), F32)] + [c_.out_shape for c_ in casts],
        compiler_params=_params("arbitrary"),
        name="out_proj",
    )(attn, c, x, w, b, g, beta, *[c_.src for c_ in casts])


def _ffn_kernel(x_ref, hu_ref, hg_ref, wu_ref, wg_ref, cwu_ref, cwg_ref, cbu_ref, cbg_ref, wdn_ref, g_ref, beta_ref,
                *refs, tm, prev, stride, tiles_per_seg, st0, st_rows, st_slabs, alpha, rb, n_split, dup_cols, n_side):
    side_in, refs = refs[:n_side], refs[n_side:]
    o_ref, su_ref, sg_ref = refs[:3]
    side_out, refs = refs[3:3 + n_side], refs[3 + n_side:]
    xb_ref, hsu_ref, hsg_ref, p_ref = refs[:4]
    carry = refs[4:]
    i = pl.program_id(0)
    j = pl.program_id(1)
    nj = pl.num_programs(1)
    fc = p_ref.shape[1]
    hm = tm // n_split

    @pl.when(j == 0)
    def _():
        xb_ref[...] = x_ref[...].astype(BF16)
        o_ref[...] = jnp.zeros_like(o_ref)
        if carry:
            @pl.when(i == 0)
            def _():
                for cr in carry:
                    cr[...] = jnp.zeros_like(cr)

    for h in range(n_split):
        xs = xb_ref[h * hm:(h + 1) * hm, :]
        hsu_ref[prev + h * hm:prev + (h + 1) * hm, :] = jnp.dot(xs, wu_ref[0], preferred_element_type=F32)
        hsg_ref[prev + h * hm:prev + (h + 1) * hm, :] = jnp.dot(xs, wg_ref[0], preferred_element_type=F32)
    _run_side_casts(side_in, side_out)

    if carry:
        first = (i % tiles_per_seg) == 0
        for hs_ref, hist_ref, cr in ((hsu_ref, hu_ref, carry[0]), (hsg_ref, hg_ref, carry[1])):
            hs_ref[0:prev, :] = jnp.where(first, hist_ref[0], cr[j])
            cr[j] = hs_ref[tm:tm + prev, :]
    else:
        hsu_ref[0:prev, :] = hu_ref[0]
        hsg_ref[0:prev, :] = hg_ref[0]
    for st_ref, hs_ref in ((su_ref, hsu_ref), (sg_ref, hsg_ref)):
        v = hs_ref[prev + st0:prev + st0 + st_rows, :]
        if dup_cols:
            shifted = jnp.concatenate([v[:, dup_cols:], jnp.zeros((st_rows, dup_cols), F32)], axis=1)
            v = jnp.where(j == nj - 1, shifted, v)
        if st_slabs:
            nb = st_rows // st_slabs
            for s in range(st_slabs):
                st_ref[:, s, :] = v[s * nb:(s + 1) * nb, :]
        else:
            st_ref[0] = v

    def conv(hs_ref, cw_ref, cb_ref, r0):
        return (cw_ref[0, 2:3, :] * hs_ref[prev + r0:prev + r0 + rb, :]
                + cw_ref[0, 1:2, :] * hs_ref[prev - stride + r0:prev - stride + r0 + rb, :]
                + cw_ref[0, 0:1, :] * hs_ref[prev - 2 * stride + r0:prev - 2 * stride + r0 + rb, :]
                + cb_ref[0])

    col = jax.lax.broadcasted_iota(jnp.int32, (rb, fc), 1)
    keep = (col >= dup_cols) | (j < nj - 1)
    for h in range(n_split):
        for r0 in range(h * hm, (h + 1) * hm, rb):
            yu = conv(hsu_ref, cwu_ref, cbu_ref, r0)
            yg = conv(hsg_ref, cwg_ref, cbg_ref, r0)
            p_ref[r0:r0 + rb, :] = jnp.where(keep, yg * jax.nn.sigmoid(yg) * yu, 0.0).astype(BF16)
        rows = slice(h * hm, (h + 1) * hm)
        o_ref[rows, :] += jnp.dot(p_ref[rows, :], wdn_ref[0], preferred_element_type=F32)

    @pl.when(j == nj - 1)
    def _():
        o_ref[...] = _layer_norm(alpha * x_ref[...] + o_ref[...], g_ref[...], beta_ref[...])


def _conv_ffn(x, hist, wup, cw, cb, wdn, g, beta, l, wl, *, tm, tiles_per_seg, stride, st0, st_rows, alpha, hist_base=0,
              st_slabs=0, side=()):
    rows, d = x.shape
    _, prev, _ = hist.shape
    d_ff = wdn.shape[1]
    fc = FF_CHUNK
    nblk = d_ff // LANES
    cblk = fc // LANES
    nj = pl.cdiv(d_ff, fc)
    dup_cols = nj * fc - d_ff
    n_tiles = rows // tm
    n_split = 2
    hm = tm // n_split
    rb = 32
    assert d_ff % LANES == 0 and tm % n_split == 0 and hm % rb == 0 and nblk >= cblk

    def blk0(j):
        return jnp.minimum(j * cblk, nblk - cblk)

    el = pl.Element
    col_u = lambda j: LANES * blk0(j)
    col_g = lambda j: LANES * (nblk + blk0(j))
    casts = [_SideCast(sw, wl, r, n_tiles * nj, lambda i, j: i * nj + j) for sw, wl, r in side]
    kern = functools.partial(_ffn_kernel, tm=tm, prev=prev, stride=stride, tiles_per_seg=tiles_per_seg, st0=st0,
                             st_rows=st_rows, st_slabs=st_slabs, alpha=alpha, rb=rb, n_split=n_split,
                             dup_cols=dup_cols, n_side=len(casts))
    scratch = [pltpu.VMEM((tm, d), BF16), pltpu.VMEM((prev + tm, fc), F32), pltpu.VMEM((prev + tm, fc), F32),
               pltpu.VMEM((tm, fc), BF16)]
    if tiles_per_seg > 1:
        scratch += [pltpu.VMEM((nj, prev, fc), F32), pltpu.VMEM((nj, prev, fc), F32)]
    hist_blk = (el(1), el(prev), el(fc))
    if st_slabs:
        assert n_tiles == 1 and st_rows % st_slabs == 0
        st_dims = (st_rows // st_slabs, st_slabs)
        st_spec = pl.BlockSpec(st_dims + (fc,), lambda i, j: (0, 0, j))
    else:
        st_dims = (n_tiles, st_rows)
        st_spec = pl.BlockSpec((1, st_rows, fc), lambda i, j: (i, 0, j))
    return pl.pallas_call(
        kern,
        grid=(n_tiles, nj),
        in_specs=[pl.BlockSpec((tm, d), lambda i, j: (i, 0)),
                  pl.BlockSpec(hist_blk, lambda i, j: (hist_base + i // tiles_per_seg, 0, col_u(j))),
                  pl.BlockSpec(hist_blk, lambda i, j: (hist_base + i // tiles_per_seg, 0, col_g(j))),
                  pl.BlockSpec((el(1), el(d), el(fc)), lambda i, j: (wl, 0, col_u(j))),
                  pl.BlockSpec((el(1), el(d), el(fc)), lambda i, j: (wl, 0, col_g(j))),
                  pl.BlockSpec((el(1), el(3), el(fc)), lambda i, j: (l, 0, col_u(j))),
                  pl.BlockSpec((el(1), el(3), el(fc)), lambda i, j: (l, 0, col_g(j))),
                  pl.BlockSpec((el(1), el(1), el(fc)), lambda i, j: (l, 0, col_u(j))),
                  pl.BlockSpec((el(1), el(1), el(fc)), lambda i, j: (l, 0, col_g(j))),
                  pl.BlockSpec((el(1), el(fc), el(d)), lambda i, j: (wl, col_u(j), 0)),
                  _layer_spec(l, 1, d), _layer_spec(l, 1, d)] + [c_.in_spec for c_ in casts],
        out_specs=[pl.BlockSpec((tm, d), lambda i, j: (i, 0)), st_spec, st_spec] + [c_.out_spec for c_ in casts],
        out_shape=[jax.ShapeDtypeStruct((rows, d), F32),
                   jax.ShapeDtypeStruct(st_dims + (nj * fc,), F32),
                   jax.ShapeDtypeStruct(st_dims + (nj * fc,), F32)] + [c_.out_shape for c_ in casts],
        scratch_shapes=scratch,
        compiler_params=_params("arbitrary", "arbitrary"),
        name="conv_ffn",
    )(x, hist, hist, wup, wup, cw, cw, cb, cb, wdn, g, beta, *[c_.src for c_ in casts])


def kernel(x_prompt, x_sample, state_attn_k, state_attn_v, state_conv, state_ffn_conv, meta_tokens, w_in, b_in, attn_sinks, conv_w, conv_b, conv_ln_g, conv_ln_b, w_out, b_out, ln1_g, ln1_b, ffn_w_up, ffn_conv_w, ffn_conv_b, ffn_w_down, ln2_g, ln2_b):
    batch, seq, d_model = x_prompt.shape
    dec_batch, dec_seq, _ = x_sample.shape
    depth = w_in.shape[0]
    conv_ch = conv_w.shape[2]
    conv_taps = conv_w.shape[1]
    ffn_taps = ffn_conv_w.shape[1]
    attn_w = d_model - conv_ch
    kv_w = 2 * N_KV_HEADS * HEAD_DIM
    n_heads = attn_w // HEAD_DIM
    group = n_heads // N_KV_HEADS
    d_ff = ffn_w_down.shape[1]
    ct = conv_ch // LANES
    alpha = (2 * depth) ** 0.25
    assert ffn_taps == 3 and kv_w == 2 * LANES and dec_batch % SUBLANES == 0
    assert ffn_taps - 1 <= dec_seq <= min(conv_taps - 1, SUBLANES) and seq >= WINDOW >= conv_taps

    seq_all = N_META + seq
    lp = _round_up(seq_all, WINDOW)
    tiles_p = 6
    tm_p = lp // tiles_p
    assert tm_p % 64 == 0
    meta = jnp.broadcast_to(meta_tokens[None].astype(x_prompt.dtype), (batch, N_META, d_model))
    xp = jnp.concatenate([meta, x_prompt, jnp.zeros((batch, lp - seq_all, d_model), x_prompt.dtype)], axis=1)
    xp = xp.reshape(batch * lp, d_model)
    rows_s = dec_seq * dec_batch
    xs = jnp.swapaxes(x_sample, 0, 1).reshape(rows_s, d_model)

    conv_hist_rows = 32
    ffn_prev_p = SUBLANES
    last_p = seq_all - 1
    ffn_state_tile = last_p // tm_p
    ffn_st0 = (last_p % tm_p) // SUBLANES * SUBLANES
    assert (last_p - 1) // tm_p == ffn_state_tile and (last_p - 1) % tm_p >= ffn_st0

    rows3 = lambda v: v[:, None, :]
    b_in3, conv_b3, cg3, cbt3 = rows3(b_in), rows3(conv_b), rows3(conv_ln_g), rows3(conv_ln_b)
    b_out3, g1, be1, g2, be2 = rows3(b_out), rows3(ln1_g), rows3(ln1_b), rows3(ln2_g), rows3(ln2_b)
    fcb3 = rows3(ffn_conv_b)
    zero_conv_hist = jnp.zeros((batch, ct, conv_hist_rows, LANES), F32)
    zero_ffn_hist = jnp.zeros((batch, ffn_prev_p, 2 * d_ff), F32)
    hk_all = state_attn_k.reshape(depth, dec_batch, WINDOW, LANES)
    hv_all = state_attn_v.reshape(depth, dec_batch, WINDOW, LANES)
    conv_hist_all = jnp.swapaxes(state_conv, 1, 2)
    st_rows = (ffn_taps - 1) * dec_batch
    ffn_hist_all = jnp.swapaxes(state_ffn_conv, 1, 2).reshape(depth, st_rows, 2 * d_ff)

    w_in_b = w_in[:1].astype(BF16)
    w_out_b = w_out[:1].astype(BF16)
    w_up_b = w_dn_b = None

    pk, pv, pc, pf, sf = [], [], [], [], []
    kv_state = conv_state = None
    for l in range(depth):
        sinks = attn_sinks[l]
        first = l == 0
        nxt = l + 1 < depth

        q, kv, u = _in_proj(xp, w_in_b, b_in3, l, 0, tm=tm_p, attn_w=attn_w, conv_ch=conv_ch, kv_w=kv_w)
        attn, = _attn_prompt(sinks, q, kv, n_seg=batch, blocks_per_seg=lp // WINDOW, blk=WINDOW)
        c, *cast = _conv_module(zero_conv_hist, u, conv_w, conv_b3, cg3, cbt3, l, tm=tm_p, tiles_per_seg=tiles_p,
                                side=[(ffn_w_up, 0, 176)] if first else [])
        if first:
            w_up_b = cast[0][None]
        x1, *cast = _out_proj(attn, c, xp, w_out_b, b_out3, g1, be1, l, 0, tm=tm_p, alpha=alpha,
                              side=[(ffn_w_down, 0, 512)] if first else [])
        if first:
            w_dn_b = cast[0][None]
        next_w = [(w_in, l + 1, 16), (w_out, l + 1, 16), (ffn_w_up, l + 1, 16), (ffn_w_down, l + 1, 48)] if nxt else []
        xp, su, sg, *cast = _conv_ffn(x1, zero_ffn_hist, w_up_b, ffn_conv_w, fcb3, w_dn_b, g2, be2, l, 0, tm=tm_p,
                                      tiles_per_seg=tiles_p, stride=1, st0=ffn_st0, st_rows=SUBLANES, alpha=alpha,
                                      side=next_w)
        kv3 = kv.reshape(batch, lp, kv_w)[:, seq_all - WINDOW:seq_all]
        pk.append(kv3[..., :kv_w // 2].reshape(batch, WINDOW, N_KV_HEADS, HEAD_DIM))
        pv.append(kv3[..., kv_w // 2:].reshape(batch, WINDOW, N_KV_HEADS, HEAD_DIM))
        u_tail = u.reshape(ct, batch, lp, LANES)[:, :, seq_all - (conv_taps - 1):seq_all]
        pc.append(u_tail.transpose(1, 2, 0, 3).reshape(batch, conv_taps - 1, conv_ch))
        off = last_p % tm_p - ffn_st0 - (ffn_taps - 2)
        hst = jnp.concatenate([su[..., :d_ff], sg[..., :d_ff]], axis=-1).reshape(batch, tiles_p, SUBLANES, 2 * d_ff)
        pf.append(hst[:, ffn_state_tile, off:off + ffn_taps - 1])

        q, kv, u = _in_proj(xs, w_in_b, b_in3, l, 0, tm=rows_s, attn_w=attn_w, conv_ch=conv_ch, kv_w=kv_w)
        q5 = q.reshape(dec_seq, dec_batch, N_KV_HEADS, group, HEAD_DIM).transpose(1, 2, 0, 3, 4)
        q5 = q5.reshape(dec_batch, N_KV_HEADS, dec_seq * group, HEAD_DIM)
        zq = jnp.zeros_like(q5[:, 0])
        qpad = jnp.concatenate([jnp.concatenate([q5[:, 0], zq], axis=-1),
                                jnp.concatenate([zq, q5[:, 1]], axis=-1)], axis=1)
        kv_new = kv.reshape(dec_seq, dec_batch, kv_w).transpose(1, 0, 2)
        kv_new_pad = jnp.pad(kv_new, ((0, 0), (0, SUBLANES - dec_seq), (0, 0)))
        o, *kv_state = _attn_sample(sinks, qpad, hk_all, hv_all, kv_new_pad, l, kv_state, t_steps=dec_seq, bb=8)
        o = (o[..., :HEAD_DIM] + o[..., HEAD_DIM:]).reshape(dec_batch, N_KV_HEADS, dec_seq, group, HEAD_DIM)
        attn = o.transpose(2, 0, 1, 3, 4).reshape(rows_s, attn_w).astype(BF16)
        u4 = u.reshape(ct, dec_seq, dec_batch, LANES)
        c, conv_state = _conv_module_sample(conv_hist_all, u4, conv_w, conv_b3, cg3, cbt3, l, conv_state, bb=32)
        c = c.reshape(rows_s, conv_ch)
        x1, = _out_proj(attn, c, xs, w_out_b, b_out3, g1, be1, l, 0, tm=rows_s, alpha=alpha)
        xs, su, sg = _conv_ffn(x1, ffn_hist_all, w_up_b, ffn_conv_w, fcb3, w_dn_b, g2, be2, l, 0, tm=rows_s,
                               tiles_per_seg=1, stride=dec_batch, st0=rows_s - st_rows, st_rows=st_rows,
                               alpha=alpha, hist_base=l, st_slabs=ffn_taps - 1)
        sf.append(jnp.concatenate([su[..., :d_ff], sg[..., :d_ff]], axis=-1))
        if nxt:
            w_in_b, w_out_b, w_up_b, w_dn_b = (w[None] for w in cast)

    y_prompt = xp.reshape(batch, lp, d_model)[:, N_META:seq_all]
    y_sample = jnp.swapaxes(xs.reshape(dec_seq, dec_batch, d_model), 0, 1)
    kv_shape = (depth, dec_batch, WINDOW, N_KV_HEADS, HEAD_DIM)
    return (y_prompt, y_sample, jnp.stack(pk), jnp.stack(pv), jnp.stack(pc), jnp.stack(pf),
            kv_state[0].reshape(kv_shape), kv_state[1].reshape(kv_shape), conv_state, jnp.stack(sf))
```

```python
import functools

import jax
import jax.numpy as jnp
from jax.experimental import pallas as pl
from jax.experimental.pallas import tpu as pltpu

F32 = jnp.float32
BF16 = jnp.bfloat16

N_META = 16
HEAD_DIM = 64
N_KV_HEADS = 2
WINDOW = 128
LN_EPS = 1e-5
LANES = 128
SUBLANES = 8
BF16_ROWS = 16
FF_CHUNK = 512
VMEM_LIMIT = 56 * 1024 * 1024


def _round_up(x, m):
    return (x + m - 1) // m * m


def _layer_norm(y, g, b):
    mu = jnp.mean(y, axis=-1, keepdims=True)
    d = y - mu
    var = jnp.mean(d * d, axis=-1, keepdims=True)
    return d * jax.lax.rsqrt(var + LN_EPS) * g + b


def _params(*sem):
    return pltpu.CompilerParams(dimension_semantics=sem, vmem_limit_bytes=VMEM_LIMIT)


def _layer_spec(l, *block):
    zeros = (0,) * len(block)
    return pl.BlockSpec((None,) + block, lambda *_: (l,) + zeros)


class _SideCast:
    def __init__(self, src, layer, chunk_rows, n_steps, step_of):
        _, rows, cols = src.shape
        n_chunks = pl.cdiv(rows, chunk_rows)
        assert n_chunks <= n_steps and chunk_rows % BF16_ROWS == 0
        self.src = src
        chunk = lambda *g: jnp.minimum(step_of(*g), n_chunks - 1)
        self.in_spec = pl.BlockSpec((None, chunk_rows, cols), lambda *g: (layer, chunk(*g), 0))
        self.out_spec = pl.BlockSpec((chunk_rows, cols), lambda *g: (chunk(*g), 0))
        self.out_shape = jax.ShapeDtypeStruct((rows, cols), BF16)


def _run_side_casts(side_in, side_out):
    for src_ref, dst_ref in zip(side_in, side_out):
        dst_ref[...] = src_ref[...].astype(BF16)


def _in_proj_kernel(x_ref, w_ref, b_ref, q_ref, kv_ref, u_ref, *, attn_w, conv_ch, kv_w, q_scale):
    xb = x_ref[...].astype(BF16)

    def proj(c0, n):
        return jnp.dot(xb, w_ref[:, c0:c0 + n], preferred_element_type=F32) + b_ref[:, c0:c0 + n]

    q_ref[...] = (proj(0, attn_w) * q_scale).astype(BF16)
    kv_ref[...] = proj(attn_w, kv_w)
    a = proj(attn_w + kv_w, conv_ch)
    g = proj(attn_w + kv_w + conv_ch, conv_ch)
    u = a * jax.nn.sigmoid(g)
    for c in range(conv_ch // LANES):
        u_ref[c] = u[:, c * LANES:(c + 1) * LANES]


def _in_proj(x, w, b, l, wl, *, tm, attn_w, conv_ch, kv_w):
    rows, d = x.shape
    n = w.shape[2]
    ct = conv_ch // LANES
    kern = functools.partial(_in_proj_kernel, attn_w=attn_w, conv_ch=conv_ch, kv_w=kv_w,
                             q_scale=HEAD_DIM ** -0.5)
    return pl.pallas_call(
        kern,
        grid=(rows // tm,),
        in_specs=[pl.BlockSpec((tm, d), lambda i: (i, 0)),
                  pl.BlockSpec((None, d, n), lambda i: (wl, 0, 0), pipeline_mode=pl.Buffered(1)),
                  _layer_spec(l, 1, n)],
        out_specs=[pl.BlockSpec((tm, attn_w), lambda i: (i, 0)),
                   pl.BlockSpec((tm, kv_w), lambda i: (i, 0)),
                   pl.BlockSpec((ct, tm, LANES), lambda i: (0, i, 0))],
        out_shape=[jax.ShapeDtypeStruct((rows, attn_w), BF16),
                   jax.ShapeDtypeStruct((rows, kv_w), F32),
                   jax.ShapeDtypeStruct((ct, rows, LANES), F32)],
        compiler_params=_params("arbitrary"),
        name="in_proj",
    )(x, w, b)


def _swish_ln_tiles(tiles, g_ref, b_ref, ch):
    total = tiles[0]
    for tl in tiles[1:]:
        total = total + tl
    mu = jnp.sum(total, axis=-1, keepdims=True) / ch
    ds = [tl - mu for tl in tiles]
    sq = ds[0] * ds[0]
    for d in ds[1:]:
        sq = sq + d * d
    inv = jax.lax.rsqrt(jnp.sum(sq, axis=-1, keepdims=True) / ch + LN_EPS)
    out = []
    for c, d in enumerate(ds):
        y = d * inv * g_ref[:, c * LANES:(c + 1) * LANES] + b_ref[:, c * LANES:(c + 1) * LANES]
        out.append((y * jax.nn.sigmoid(y)).astype(BF16))
    return out


def _conv_kernel(hist_ref, u_ref, w_ref, cb_ref, g_ref, b_ref, *refs, tm, taps, qb, n_side):
    c_ref = refs[n_side]
    ext_ref, tail_ref, y_ref = refs[2 * n_side + 1:]
    _run_side_casts(refs[:n_side], refs[n_side + 1:2 * n_side + 1])
    ct = u_ref.shape[0]
    ch = ct * LANES
    nq = tm // SUBLANES
    nb = taps - 1
    hb = hist_ref.shape[2]
    t = pl.program_id(1)

    @pl.when(t == 0)
    def _():
        for c in range(ct):
            for e in range(nb):
                row = hist_ref[0, c, hb - nb + e:hb - nb + e + 1, :]
                tail_ref[c, e * SUBLANES:(e + 1) * SUBLANES, :] = jnp.broadcast_to(row, (SUBLANES, LANES))

    @pl.when(t > 0)
    def _():
        tail_ref[...] = ext_ref[:, nq * SUBLANES:(nq + nb) * SUBLANES, :]

    def load(q, carry):
        dst = pl.multiple_of((nb + q) * SUBLANES, SUBLANES)
        for c in range(ct):
            ext_ref[c, pl.ds(dst, SUBLANES), :] = u_ref[c, pl.ds(q, SUBLANES, stride=nq), :]
        return carry

    jax.lax.fori_loop(0, nq, load, 0)

    first_strip = jax.lax.broadcasted_iota(jnp.int32, (SUBLANES, LANES), 0) == 0
    for c in range(ct):
        for e in range(nb):
            cur = ext_ref[c, (nq + e) * SUBLANES:(nq + e + 1) * SUBLANES, :]
            prv = tail_ref[c, e * SUBLANES:(e + 1) * SUBLANES, :]
            ext_ref[c, e * SUBLANES:(e + 1) * SUBLANES, :] = jnp.where(
                first_strip, pltpu.roll(prv, 1, 0), pltpu.roll(cur, 1, 0))

    def conv_block(ib, carry):
        rows = qb * SUBLANES
        for c in range(ct):
            lanes = slice(c * LANES, (c + 1) * LANES)
            acc = w_ref[0:1, lanes] * ext_ref[c, pl.ds(pl.multiple_of(ib * rows, rows), rows), :]
            for j in range(1, taps):
                src = pl.multiple_of(ib * rows + j * SUBLANES, SUBLANES)
                acc = acc + w_ref[j:j + 1, lanes] * ext_ref[c, pl.ds(src, rows), :]
            y_ref[c, pl.ds(pl.multiple_of(ib * rows, rows), rows), :] = acc + cb_ref[:, lanes]
        return carry

    jax.lax.fori_loop(0, nq // qb, conv_block, 0)

    for r0 in range(0, tm, BF16_ROWS):
        tiles = []
        for c in range(ct):
            halves = []
            for rr in range(r0, r0 + BF16_ROWS, SUBLANES):
                s, q = divmod(rr, nq)
                halves.append(y_ref[c, pl.ds(q * SUBLANES + s, SUBLANES, stride=SUBLANES), :])
            tiles.append(jnp.concatenate(halves, axis=0))
        for c, o in enumerate(_swish_ln_tiles(tiles, g_ref, b_ref, ch)):
            c_ref[r0:r0 + BF16_ROWS, c * LANES:(c + 1) * LANES] = o


def _conv_module(hist, u, w, cb, g, b, l, *, tm, tiles_per_seg, side=()):
    ct, rows, _ = u.shape
    ch = ct * LANES
    n_seg, _, hb, _ = hist.shape
    taps = w.shape[1]
    nq = tm // SUBLANES
    qb = 8
    assert nq % qb == 0 and hb >= taps - 1 and nq >= taps - 1
    casts = [_SideCast(sw, sl, r, n_seg * tiles_per_seg, lambda s, t: s * tiles_per_seg + t) for sw, sl, r in side]
    kern = functools.partial(_conv_kernel, tm=tm, taps=taps, qb=qb, n_side=len(casts))
    return pl.pallas_call(
        kern,
        grid=(n_seg, tiles_per_seg),
        in_specs=[pl.BlockSpec((1, ct, hb, LANES), lambda s, t: (s, 0, 0, 0)),
                  pl.BlockSpec((ct, tm, LANES), lambda s, t: (0, s * tiles_per_seg + t, 0)),
                  _layer_spec(l, taps, ch), _layer_spec(l, 1, ch), _layer_spec(l, 1, ch), _layer_spec(l, 1, ch)]
        + [c_.in_spec for c_ in casts],
        out_specs=[pl.BlockSpec((tm, ch), lambda s, t: (s * tiles_per_seg + t, 0))] + [c_.out_spec for c_ in casts],
        out_shape=[jax.ShapeDtypeStruct((rows, ch), BF16)] + [c_.out_shape for c_ in casts],
        scratch_shapes=[pltpu.VMEM((ct, tm + (taps - 1) * SUBLANES, LANES), F32),
                        pltpu.VMEM((ct, (taps - 1) * SUBLANES, LANES), F32),
                        pltpu.VMEM((ct, tm, LANES), F32)],
        compiler_params=_params("arbitrary", "arbitrary"),
        name="conv_module",
    )(hist, u, w, cb, g, b, *[c_.src for c_ in casts])


def _conv_sample_kernel(hist_ref, u_ref, w_ref, cb_ref, g_ref, b_ref, *refs):
    c_ref, ns_ref = refs[-2:]
    n_hist = hist_ref.shape[0]
    ct, t_steps = u_ref.shape[0], u_ref.shape[1]
    for k in range(ns_ref.shape[0]):
        for s in range(n_hist):
            src = s + t_steps
            if src < n_hist:
                ns_ref[k, :, s, :] = hist_ref[src]
            else:
                for c in range(ct):
                    ns_ref[k, :, s, c * LANES:(c + 1) * LANES] = u_ref[c, src - n_hist]
    for t in range(t_steps):
        tiles = []
        for c in range(ct):
            lanes = slice(c * LANES, (c + 1) * LANES)
            acc = None
            for j in range(w_ref.shape[0]):
                s = t + j
                src = hist_ref[s, :, lanes] if s < n_hist else u_ref[c, s - n_hist]
                term = w_ref[j:j + 1, lanes] * src
                acc = term if acc is None else acc + term
            tiles.append(acc + cb_ref[:, lanes])
        for c, o in enumerate(_swish_ln_tiles(tiles, g_ref, b_ref, ct * LANES)):
            c_ref[t, :, c * LANES:(c + 1) * LANES] = o


def _state_out(bufs, l, depth, block, n_inputs, first_out):
    tail = (0,) * (len(block) - 1)
    if bufs is None:
        return pl.BlockSpec((depth,) + block, lambda i: (0, i) + tail), [], [], {}
    specs = [pl.BlockSpec(memory_space=pl.ANY)] * len(bufs)
    aliases = {n_inputs + k: first_out + k for k in range(len(bufs))}
    return pl.BlockSpec((1,) + block, lambda i: (l, i) + tail), list(bufs), specs, aliases


def _conv_module_sample(hist, u, w, cb, g, b, l, state_buf, *, bb):
    depth, n_hist, n_seq, ch = hist.shape
    ct, t_steps, _, _ = u.shape
    taps = w.shape[1]
    assert n_hist == taps - 1 and ch == ct * LANES
    st_spec, bufs, buf_specs, aliases = _state_out(None if state_buf is None else [state_buf], l, depth,
                                                   (bb, n_hist, ch), 6, 1)
    return pl.pallas_call(
        _conv_sample_kernel,
        grid=(n_seq // bb,),
        in_specs=[pl.BlockSpec((None, n_hist, bb, ch), lambda i: (l, 0, i, 0)),
                  pl.BlockSpec((ct, t_steps, bb, LANES), lambda i: (0, 0, i, 0)),
                  _layer_spec(l, taps, ch), _layer_spec(l, 1, ch), _layer_spec(l, 1, ch), _layer_spec(l, 1, ch)]
        + buf_specs,
        out_specs=[pl.BlockSpec((t_steps, bb, ch), lambda i: (0, i, 0)), st_spec],
        out_shape=[jax.ShapeDtypeStruct((t_steps, n_seq, ch), BF16),
                   jax.ShapeDtypeStruct((depth, n_seq, n_hist, ch), F32)],
        input_output_aliases=aliases,
        compiler_params=_params("arbitrary"),
        name="conv_module_sample",
    )(hist, u, w, cb, g, b, *bufs)


def _attn_prompt_kernel(sink_ref, q_ref, kvp_ref, kvc_ref, *refs, n_heads, group):
    n_side = (len(refs) - 1) // 2
    o_ref = refs[n_side]
    _run_side_casts(refs[:n_side], refs[n_side + 1:])
    blk = q_ref.shape[0]
    i = pl.program_id(1)
    kv = jnp.concatenate([kvp_ref[...], kvc_ref[...]], axis=0)
    kband = kv[:, 0:LANES]
    vband = kv[:, LANES:2 * LANES]
    kroll = pltpu.roll(kband, HEAD_DIM, 1)
    vroll = pltpu.roll(vband, HEAD_DIM, 1)
    lane = jax.lax.broadcasted_iota(jnp.int32, (2 * blk, LANES), 1)
    lo = lane < HEAD_DIM

    def lo_hi(x_lo, x_hi):
        return jnp.concatenate([jnp.where(lo, x_lo, 0.0), jnp.where(lo, 0.0, x_hi)], axis=0).astype(BF16)

    kab = [lo_hi(kband, kroll), lo_hi(kroll, kband)]
    vab = [lo_hi(vband, vroll), lo_hi(vroll, vband)]

    r = jax.lax.broadcasted_iota(jnp.int32, (blk, 2 * blk), 0)
    c = jax.lax.broadcasted_iota(jnp.int32, (blk, 2 * blk), 1)
    mask = (c >= r) & (c <= r + WINDOW) & ((c >= blk) | (i > 0))
    out_lo = jax.lax.broadcasted_iota(jnp.int32, (blk, LANES), 1) < HEAD_DIM

    pairs_per_kv = group // 2
    for kvh in range(n_heads // group):
        pairs = range(kvh * pairs_per_kv, (kvh + 1) * pairs_per_kv)
        q4 = jnp.concatenate([q_ref[:, p * LANES:(p + 1) * LANES] for p in pairs], axis=0)
        s4 = jax.lax.dot_general(q4, kab[kvh], (((1,), (1,)), ((), ())), preferred_element_type=F32)
        p_rows, inv_rows = [], []
        for n, pair in enumerate(pairs):
            ps, invs = [], []
            for half in range(2):
                sink = sink_ref[2 * pair + half]
                s = jnp.where(mask, s4[n * blk:(n + 1) * blk, half * 2 * blk:(half + 1) * 2 * blk], -jnp.inf)
                m = jnp.maximum(jnp.max(s, axis=-1, keepdims=True), sink)
                p = jnp.exp(s - m)
                denom = jnp.sum(p, axis=-1, keepdims=True) + jnp.exp(sink - m)
                ps.append(p.astype(BF16))
                invs.append(1.0 / denom)
            p_rows.append(jnp.concatenate(ps, axis=1))
            inv_rows.append(jnp.where(out_lo, invs[0], invs[1]))
        o4 = jnp.dot(jnp.concatenate(p_rows, axis=0), vab[kvh], preferred_element_type=F32)
        for n, pair in enumerate(pairs):
            o_ref[:, pair * LANES:(pair + 1) * LANES] = (o4[n * blk:(n + 1) * blk] * inv_rows[n]).astype(BF16)


def _attn_prompt(sinks, q, kv, *, n_seg, blocks_per_seg, blk, side=()):
    rows, attn_w = q.shape
    casts = [_SideCast(w, wl, r, n_seg * blocks_per_seg, lambda s, i: s * blocks_per_seg + i) for w, wl, r in side]
    kv_w = kv.shape[1]
    n_heads = attn_w // HEAD_DIM
    kern = functools.partial(_attn_prompt_kernel, n_heads=n_heads, group=n_heads // N_KV_HEADS)
    return pl.pallas_call(
        kern,
        grid=(n_seg, blocks_per_seg),
        in_specs=[pl.BlockSpec(memory_space=pltpu.SMEM),
                  pl.BlockSpec((blk, attn_w), lambda s, i: (s * blocks_per_seg + i, 0)),
                  pl.BlockSpec((blk, kv_w), lambda s, i: (jnp.maximum(s * blocks_per_seg + i - 1, 0), 0)),
                  pl.BlockSpec((blk, kv_w), lambda s, i: (s * blocks_per_seg + i, 0))]
        + [c.in_spec for c in casts],
        out_specs=[pl.BlockSpec((blk, attn_w), lambda s, i: (s * blocks_per_seg + i, 0))]
        + [c.out_spec for c in casts],
        out_shape=[jax.ShapeDtypeStruct((rows, attn_w), BF16)] + [c.out_shape for c in casts],
        compiler_params=_params("arbitrary", "arbitrary"),
        name="attn_prompt",
    )(sinks, q, kv, kv, *[c.src for c in casts])


def _attn_sample_kernel(sink_ref, q_ref, hk_ref, hv_ref, nkv_ref, *refs, t_steps, group):
    o_ref, nk_ref, nv_ref = refs[-3:]
    bb, nq, _ = q_ref.shape
    npad = nkv_ref.shape[1]
    for st_ref, h_ref, lane0 in ((nk_ref, hk_ref, 0), (nv_ref, hv_ref, LANES)):
        for k in range(st_ref.shape[0]):
            st_ref[k, :, 0:WINDOW - t_steps, :] = h_ref[:, t_steps:WINDOW, :]
            st_ref[k, :, WINDOW - t_steps:WINDOW, :] = nkv_ref[:, 0:t_steps, lane0:lane0 + LANES]
    q = q_ref[...]
    hk = hk_ref[...].astype(BF16)
    hv = hv_ref[...].astype(BF16)
    nk = nkv_ref[:, :, 0:LANES].astype(BF16)
    nv = nkv_ref[:, :, LANES:2 * LANES].astype(BF16)
    s_h = jnp.einsum('bqd,bkd->bqk', q, hk, preferred_element_type=F32)
    s_n = jnp.einsum('bqd,bkd->bqk', q, nk, preferred_element_type=F32)
    row = jax.lax.broadcasted_iota(jnp.int32, (1, nq, 1), 1)
    t = (row // group) % t_steps
    kvh = row // (group * t_steps)
    head = kvh * group + row % group
    sink = jnp.zeros((1, nq, 1), F32)
    for h in range(N_KV_HEADS * group):
        sink = jnp.where(head == h, sink_ref[h], sink)
    j_h = jax.lax.broadcasted_iota(jnp.int32, (1, nq, WINDOW), 2)
    j_n = jax.lax.broadcasted_iota(jnp.int32, (1, nq, npad), 2)
    s_h = jnp.where(j_h >= t, s_h, -jnp.inf)
    s_n = jnp.where(j_n <= t, s_n, -jnp.inf)
    m = jnp.maximum(jnp.maximum(jnp.max(s_h, axis=-1, keepdims=True), jnp.max(s_n, axis=-1, keepdims=True)), sink)
    p_h = jnp.exp(s_h - m)
    p_n = jnp.exp(s_n - m)
    denom = jnp.sum(p_h, axis=-1, keepdims=True) + jnp.sum(p_n, axis=-1, keepdims=True) + jnp.exp(sink - m)
    o = (jnp.einsum('bqk,bkd->bqd', p_h.astype(BF16), hv, preferred_element_type=F32)
         + jnp.einsum('bqk,bkd->bqd', p_n.astype(BF16), nv, preferred_element_type=F32))
    lane_kvh = jax.lax.broadcasted_iota(jnp.int32, (1, nq, LANES), 2) // HEAD_DIM
    o_ref[...] = jnp.where(lane_kvh == kvh, o / denom, 0.0)


def _attn_sample(sinks, qpad, hist_k, hist_v, new_kv, l, state_bufs, *, t_steps, bb):
    n_seq, nq, _ = qpad.shape
    npad = new_kv.shape[1]
    n_heads = sinks.shape[0]
    kern = functools.partial(_attn_sample_kernel, t_steps=t_steps, group=n_heads // N_KV_HEADS)
    st_spec, bufs, buf_specs, aliases = _state_out(state_bufs, l, hist_k.shape[0], (bb, WINDOW, LANES), 5, 1)
    st_shape = jax.ShapeDtypeStruct(hist_k.shape, F32)
    return pl.pallas_call(
        kern,
        grid=(n_seq // bb,),
        in_specs=[pl.BlockSpec(memory_space=pltpu.SMEM),
                  pl.BlockSpec((bb, nq, LANES), lambda i: (i, 0, 0)),
                  pl.BlockSpec((None, bb, WINDOW, LANES), lambda i: (l, i, 0, 0)),
                  pl.BlockSpec((None, bb, WINDOW, LANES), lambda i: (l, i, 0, 0)),
                  pl.BlockSpec((bb, npad, 2 * LANES), lambda i: (i, 0, 0))] + buf_specs,
        out_specs=[pl.BlockSpec((bb, nq, LANES), lambda i: (i, 0, 0)), st_spec, st_spec],
        out_shape=[jax.ShapeDtypeStruct((n_seq, nq, LANES), F32), st_shape, st_shape],
        input_output_aliases=aliases,
        compiler_params=_params("arbitrary"),
        name="attn_sample",
    )(sinks, qpad, hist_k, hist_v, new_kv, *bufs)


def _out_proj_kernel(a_ref, c_ref, x_ref, w_ref, b_ref, g_ref, beta_ref, *refs, alpha):
    n_side = (len(refs) - 1) // 2
    o_ref = refs[n_side]
    _run_side_casts(refs[:n_side], refs[n_side + 1:])
    aw = a_ref.shape[1]
    hm = a_ref.shape[0] // 2
    for h in range(2):
        rows = slice(h * hm, (h + 1) * hm)
        mix = (jnp.dot(a_ref[rows, :], w_ref[0:aw, :], preferred_element_type=F32)
               + jnp.dot(c_ref[rows, :], w_ref[aw:, :], preferred_element_type=F32) + b_ref[...])
        o_ref[rows, :] = _layer_norm(alpha * x_ref[rows, :] + mix, g_ref[...], beta_ref[...])


def _out_proj(attn, c, x, w, b, g, beta, l, wl, *, tm, alpha, side=()):
    rows, d = x.shape
    aw, cw = attn.shape[1], c.shape[1]
    casts = [_SideCast(sw, wl, r, rows // tm, lambda i: i) for sw, wl, r in side]
    return pl.pallas_call(
        functools.partial(_out_proj_kernel, alpha=alpha),
        grid=(rows // tm,),
        in_specs=[pl.BlockSpec((tm, aw), lambda i: (i, 0)),
                  pl.BlockSpec((tm, cw), lambda i: (i, 0)),
                  pl.BlockSpec((tm, d), lambda i: (i, 0)),
                  pl.BlockSpec((None, aw + cw, d), lambda i: (wl, 0, 0), pipeline_mode=pl.Buffered(1)),
                  _layer_spec(l, 1, d), _layer_spec(l, 1, d), _layer_spec(l, 1, d)] + [c_.in_spec for c_ in casts],
        out_specs=[pl.BlockSpec((tm, d), lambda i: (i, 0))] + [c_.out_spec for c_ in casts],
        out_shape=[jax.ShapeDtypeStruct((rows, d), F32)] + [c_.out_shape for c_ in casts],
        compiler_params=_params("arbitrary"),
        name="out_proj",
    )(attn, c, x, w, b, g, beta, *[c_.src for c_ in casts])


def _ffn_kernel(x_ref, hu_ref, hg_ref, wu_ref, wg_ref, cwu_ref, cwg_ref, cbu_ref, cbg_ref, wdn_ref, g_ref, beta_ref,
                *refs, tm, prev, stride, tiles_per_seg, st0, st_rows, st_slabs, alpha, rb, n_split, dup_cols, n_side,
                halo):
    side_in, refs = refs[:n_side], refs[n_side:]
    o_ref, su_ref, sg_ref = refs[:3]
    side_out, refs = refs[3:3 + n_side], refs[3 + n_side:]
    xb_ref, hsu_ref, hsg_ref, p_ref = refs[:4]
    carry = refs[4:]
    i = pl.program_id(0)
    j = pl.program_id(1)
    nj = pl.num_programs(1)
    fc = p_ref.shape[1]
    hm = tm // n_split

    @pl.when(j == 0)
    def _():
        xb_ref[...] = x_ref[...].astype(BF16)
        o_ref[...] = jnp.zeros_like(o_ref)
        if carry:
            @pl.when(i == 0)
            def _():
                for cr in carry:
                    cr[...] = jnp.zeros_like(cr)

    for h in range(n_split):
        if halo:
            lo, hi = (0 if h == 0 else prev + h * hm), prev + (h + 1) * hm
            xs = xb_ref[lo:hi, :]
        else:
            lo, hi = prev + h * hm, prev + (h + 1) * hm
            xs = xb_ref[h * hm:(h + 1) * hm, :]
        hsu_ref[lo:hi, :] = jnp.dot(xs, wu_ref[0], preferred_element_type=F32)
        hsg_ref[lo:hi, :] = jnp.dot(xs, wg_ref[0], preferred_element_type=F32)
    _run_side_casts(side_in, side_out)

    if halo:
        pass
    elif carry:
        first = (i % tiles_per_seg) == 0
        for hs_ref, hist_ref, cr in ((hsu_ref, hu_ref, carry[0]), (hsg_ref, hg_ref, carry[1])):
            hs_ref[0:prev, :] = jnp.where(first, hist_ref[0], cr[j])
            cr[j] = hs_ref[tm:tm + prev, :]
    else:
        hsu_ref[0:prev, :] = hu_ref[0]
        hsg_ref[0:prev, :] = hg_ref[0]
    for st_ref, hs_ref in ((su_ref, hsu_ref), (sg_ref, hsg_ref)):
        v = hs_ref[prev + st0:prev + st0 + st_rows, :]
        if dup_cols:
            shifted = jnp.concatenate([v[:, dup_cols:], jnp.zeros((st_rows, dup_cols), F32)], axis=1)
            v = jnp.where(j == nj - 1, shifted, v)
        if st_slabs:
            nb = st_rows // st_slabs
            for s in range(st_slabs):
                st_ref[:, s, :] = v[s * nb:(s + 1) * nb, :]
        else:
            st_ref[0] = v

    def conv(hs_ref, cw_ref, cb_ref, r0):
        return (cw_ref[0, 2:3, :] * hs_ref[prev + r0:prev + r0 + rb, :]
                + cw_ref[0, 1:2, :] * hs_ref[prev - stride + r0:prev - stride + r0 + rb, :]
                + cw_ref[0, 0:1, :] * hs_ref[prev - 2 * stride + r0:prev - 2 * stride + r0 + rb, :]
                + cb_ref[0])

    col = jax.lax.broadcasted_iota(jnp.int32, (rb, fc), 1)
    keep = (col >= dup_cols) | (j < nj - 1)
    for h in range(n_split):
        for r0 in range(h * hm, (h + 1) * hm, rb):
            yu = conv(hsu_ref, cwu_ref, cbu_ref, r0)
            yg = conv(hsg_ref, cwg_ref, cbg_ref, r0)
            p_ref[r0:r0 + rb, :] = jnp.where(keep, yg * jax.nn.sigmoid(yg) * yu, 0.0).astype(BF16)
        rows = slice(h * hm, (h + 1) * hm)
        o_ref[rows, :] += jnp.dot(p_ref[rows, :], wdn_ref[0], preferred_element_type=F32)

    @pl.when(j == nj - 1)
    def _():
        x_rows = x_ref[prev:prev + tm, :] if halo else x_ref[...]
        o_ref[...] = _layer_norm(alpha * x_rows + o_ref[...], g_ref[...], beta_ref[...])


def _conv_ffn(x, hist, wup, cw, cb, wdn, g, beta, l, wl, *, tm, tiles_per_seg, stride, st0, st_rows, alpha, hist_base=0,
              st_slabs=0, side=(), fc=FF_CHUNK, seg_rows=None):
    halo = seg_rows is not None
    d = x.shape[1]
    _, prev, _ = hist.shape
    d_ff = wdn.shape[1]
    nblk = d_ff // LANES
    cblk = fc // LANES
    nj = pl.cdiv(d_ff, fc)
    dup_cols = nj * fc - d_ff
    n_tiles = x.shape[0] // seg_rows * tiles_per_seg if halo else x.shape[0] // tm
    rows = n_tiles * tm
    n_split = 2
    hm = tm // n_split
    rb = 32
    assert d_ff % LANES == 0 and tm % n_split == 0 and hm % rb == 0 and nblk >= cblk

    def blk0(j):
        return jnp.minimum(j * cblk, nblk - cblk)

    el = pl.Element
    col_u = lambda j: LANES * blk0(j)
    col_g = lambda j: LANES * (nblk + blk0(j))
    casts = [_SideCast(sw, wl, r, n_tiles * nj, lambda i, j: i * nj + j) for sw, wl, r in side]
    kern = functools.partial(_ffn_kernel, tm=tm, prev=prev, stride=stride, tiles_per_seg=tiles_per_seg, st0=st0,
                             st_rows=st_rows, st_slabs=st_slabs, alpha=alpha, rb=rb, n_split=n_split,
                             dup_cols=dup_cols, n_side=len(casts), halo=halo)
    x_rows = tm + prev if halo else tm
    scratch = [pltpu.VMEM((x_rows, d), BF16), pltpu.VMEM((prev + tm, fc), F32), pltpu.VMEM((prev + tm, fc), F32),
               pltpu.VMEM((tm, fc), BF16)]
    if tiles_per_seg > 1 and not halo:
        scratch += [pltpu.VMEM((nj, prev, fc), F32), pltpu.VMEM((nj, prev, fc), F32)]
    if halo:
        assert prev % BF16_ROWS == 0 and seg_rows % BF16_ROWS == 0 and tm % BF16_ROWS == 0
        assert prev + tiles_per_seg * tm <= seg_rows
        x_spec = pl.BlockSpec((el(x_rows), el(d)), lambda i, j: (
            BF16_ROWS * ((i // tiles_per_seg) * (seg_rows // BF16_ROWS) + (i % tiles_per_seg) * (tm // BF16_ROWS)), 0))
    else:
        x_spec = pl.BlockSpec((tm, d), lambda i, j: (i, 0))
    hist_blk = (el(1), el(prev), el(fc))
    if st_slabs:
        assert n_tiles == 1 and st_rows % st_slabs == 0
        st_dims = (st_rows // st_slabs, st_slabs)
        st_spec = pl.BlockSpec(st_dims + (fc,), lambda i, j: (0, 0, j))
    else:
        st_dims = (n_tiles, st_rows)
        st_spec = pl.BlockSpec((1, st_rows, fc), lambda i, j: (i, 0, j))
    return pl.pallas_call(
        kern,
        grid=(n_tiles, nj),
        in_specs=[x_spec,
                  pl.BlockSpec(hist_blk, lambda i, j: (hist_base + i // tiles_per_seg, 0, col_u(j))),
                  pl.BlockSpec(hist_blk, lambda i, j: (hist_base + i // tiles_per_seg, 0, col_g(j))),
                  pl.BlockSpec((el(1), el(d), el(fc)), lambda i, j: (wl, 0, col_u(j))),
                  pl.BlockSpec((el(1), el(d), el(fc)), lambda i, j: (wl, 0, col_g(j))),
                  pl.BlockSpec((el(1), el(3), el(fc)), lambda i, j: (l, 0, col_u(j))),
                  pl.BlockSpec((el(1), el(3), el(fc)), lambda i, j: (l, 0, col_g(j))),
                  pl.BlockSpec((el(1), el(1), el(fc)), lambda i, j: (l, 0, col_u(j))),
                  pl.BlockSpec((el(1), el(1), el(fc)), lambda i, j: (l, 0, col_g(j))),
                  pl.BlockSpec((el(1), el(fc), el(d)), lambda i, j: (wl, col_u(j), 0)),
                  _layer_spec(l, 1, d), _layer_spec(l, 1, d)] + [c_.in_spec for c_ in casts],
        out_specs=[pl.BlockSpec((tm, d), lambda i, j: (i, 0)), st_spec, st_spec] + [c_.out_spec for c_ in casts],
        out_shape=[jax.ShapeDtypeStruct((rows, d), F32),
                   jax.ShapeDtypeStruct(st_dims + (nj * fc,), F32),
                   jax.ShapeDtypeStruct(st_dims + (nj * fc,), F32)] + [c_.out_shape for c_ in casts],
        scratch_shapes=scratch,
        compiler_params=_params("arbitrary", "arbitrary"),
        name="conv_ffn",
    )(x, hist, hist, wup, wup, cw, cw, cb, cb, wdn, g, beta, *[c_.src for c_ in casts])


def kernel(x_prompt, x_sample, state_attn_k, state_attn_v, state_conv, state_ffn_conv, meta_tokens, w_in, b_in, attn_sinks, conv_w, conv_b, conv_ln_g, conv_ln_b, w_out, b_out, ln1_g, ln1_b, ffn_w_up, ffn_conv_w, ffn_conv_b, ffn_w_down, ln2_g, ln2_b):
    batch, seq, d_model = x_prompt.shape
    dec_batch, dec_seq, _ = x_sample.shape
    depth = w_in.shape[0]
    conv_ch = conv_w.shape[2]
    conv_taps = conv_w.shape[1]
    ffn_taps = ffn_conv_w.shape[1]
    attn_w = d_model - conv_ch
    kv_w = 2 * N_KV_HEADS * HEAD_DIM
    n_heads = attn_w // HEAD_DIM
    group = n_heads // N_KV_HEADS
    d_ff = ffn_w_down.shape[1]
    ct = conv_ch // LANES
    alpha = (2 * depth) ** 0.25
    assert ffn_taps == 3 and kv_w == 2 * LANES and dec_batch % SUBLANES == 0
    assert ffn_taps - 1 <= dec_seq <= min(conv_taps - 1, SUBLANES) and seq >= WINDOW >= conv_taps

    seq_all = N_META + seq
    lp = _round_up(seq_all, WINDOW)
    tiles_p = 6
    tm_p = lp // tiles_p
    assert tm_p % 64 == 0
    meta = jnp.broadcast_to(meta_tokens[None].astype(x_prompt.dtype), (batch, N_META, d_model))
    xp = jnp.concatenate([meta, x_prompt, jnp.zeros((batch, lp - seq_all, d_model), x_prompt.dtype)], axis=1)
    xp = xp.reshape(batch * lp, d_model)
    rows_s = dec_seq * dec_batch
    xs = jnp.swapaxes(x_sample, 0, 1).reshape(rows_s, d_model)

    conv_hist_rows = 32
    ffn_prev_p = SUBLANES
    last_p = seq_all - 1
    ffn_state_tile = last_p // tm_p
    ffn_st0 = (last_p % tm_p) // SUBLANES * SUBLANES
    assert (last_p - 1) // tm_p == ffn_state_tile and (last_p - 1) % tm_p >= ffn_st0

    rows3 = lambda v: v[:, None, :]
    b_in3, conv_b3, cg3, cbt3 = rows3(b_in), rows3(conv_b), rows3(conv_ln_g), rows3(conv_ln_b)
    b_out3, g1, be1, g2, be2 = rows3(b_out), rows3(ln1_g), rows3(ln1_b), rows3(ln2_g), rows3(ln2_b)
    fcb3 = rows3(ffn_conv_b)
    zero_conv_hist = jnp.zeros((batch, ct, conv_hist_rows, LANES), F32)
    zero_ffn_hist = jnp.zeros((batch, ffn_prev_p, 2 * d_ff), F32)
    tm_y = 512
    tiles_y = seq // tm_y
    assert seq % tm_y == 0 and N_META % BF16_ROWS == 0
    zero_ffn_halo = jnp.zeros((batch, N_META, 2 * d_ff), F32)
    hk_all = state_attn_k.reshape(depth, dec_batch, WINDOW, LANES)
    hv_all = state_attn_v.reshape(depth, dec_batch, WINDOW, LANES)
    conv_hist_all = jnp.swapaxes(state_conv, 1, 2)
    st_rows = (ffn_taps - 1) * dec_batch
    ffn_hist_all = jnp.swapaxes(state_ffn_conv, 1, 2).reshape(depth, st_rows, 2 * d_ff)

    w_in_b = w_in[:1].astype(BF16)
    w_out_b = w_out[:1].astype(BF16)
    w_up_b = w_dn_b = None

    pk, pv, pc, pf, sf = [], [], [], [], []
    kv_state = conv_state = None
    for l in range(depth):
        sinks = attn_sinks[l]
        first = l == 0
        nxt = l + 1 < depth

        q, kv, u = _in_proj(xp, w_in_b, b_in3, l, 0, tm=tm_p, attn_w=attn_w, conv_ch=conv_ch, kv_w=kv_w)
        attn, = _attn_prompt(sinks, q, kv, n_seg=batch, blocks_per_seg=lp // WINDOW, blk=WINDOW)
        c, *cast = _conv_module(zero_conv_hist, u, conv_w, conv_b3, cg3, cbt3, l, tm=tm_p, tiles_per_seg=tiles_p,
                                side=[(ffn_w_up, 0, 176)] if first else [])
        if first:
            w_up_b = cast[0][None]
        x1, *cast = _out_proj(attn, c, xp, w_out_b, b_out3, g1, be1, l, 0, tm=tm_p, alpha=alpha,
                              side=[(ffn_w_down, 0, 512)] if first else [])
        if first:
            w_dn_b = cast[0][None]
        if nxt:
            next_w = [(w_in, l + 1, 16), (w_out, l + 1, 16), (ffn_w_up, l + 1, 16), (ffn_w_down, l + 1, 48)]
            xp, su, sg, *cast = _conv_ffn(x1, zero_ffn_hist, w_up_b, ffn_conv_w, fcb3, w_dn_b, g2, be2, l, 0, tm=tm_p,
                                          tiles_per_seg=tiles_p, stride=1, st0=ffn_st0, st_rows=SUBLANES,
                                          alpha=alpha, side=next_w)
            st_tile, st_off, st_tiles = ffn_state_tile, last_p % tm_p - ffn_st0 - (ffn_taps - 2), tiles_p
        else:
            y_rows, su, sg = _conv_ffn(x1, zero_ffn_halo, w_up_b, ffn_conv_w, fcb3, w_dn_b, g2, be2, l, 0, tm=tm_y,
                                       tiles_per_seg=tiles_y, stride=1, st0=tm_y - SUBLANES, st_rows=SUBLANES,
                                       alpha=alpha, seg_rows=lp)
            st_tile, st_off, st_tiles = tiles_y - 1, SUBLANES - (ffn_taps - 1), tiles_y
        kv3 = kv.reshape(batch, lp, kv_w)[:, seq_all - WINDOW:seq_all]
        pk.append(kv3[..., :kv_w // 2].reshape(batch, WINDOW, N_KV_HEADS, HEAD_DIM))
        pv.append(kv3[..., kv_w // 2:].reshape(batch, WINDOW, N_KV_HEADS, HEAD_DIM))
        u_tail = u.reshape(ct, batch, lp, LANES)[:, :, seq_all - (conv_taps - 1):seq_all]
        pc.append(u_tail.transpose(1, 2, 0, 3).reshape(batch, conv_taps - 1, conv_ch))
        hst = jnp.concatenate([su[..., :d_ff], sg[..., :d_ff]], axis=-1).reshape(batch, st_tiles, SUBLANES, 2 * d_ff)
        pf.append(hst[:, st_tile, st_off:st_off + ffn_taps - 1])

        q, kv, u = _in_proj(xs, w_in_b, b_in3, l, 0, tm=rows_s, attn_w=attn_w, conv_ch=conv_ch, kv_w=kv_w)
        q5 = q.reshape(dec_seq, dec_batch, N_KV_HEADS, group, HEAD_DIM).transpose(1, 2, 0, 3, 4)
        q5 = q5.reshape(dec_batch, N_KV_HEADS, dec_seq * group, HEAD_DIM)
        zq = jnp.zeros_like(q5[:, 0])
        qpad = jnp.concatenate([jnp.concatenate([q5[:, 0], zq], axis=-1),
                                jnp.concatenate([zq, q5[:, 1]], axis=-1)], axis=1)
        kv_new = kv.reshape(dec_seq, dec_batch, kv_w).transpose(1, 0, 2)
        kv_new_pad = jnp.pad(kv_new, ((0, 0), (0, SUBLANES - dec_seq), (0, 0)))
        o, *kv_state = _attn_sample(sinks, qpad, hk_all, hv_all, kv_new_pad, l, kv_state, t_steps=dec_seq, bb=8)
        o = (o[..., :HEAD_DIM] + o[..., HEAD_DIM:]).reshape(dec_batch, N_KV_HEADS, dec_seq, group, HEAD_DIM)
        attn = o.transpose(2, 0, 1, 3, 4).reshape(rows_s, attn_w).astype(BF16)
        u4 = u.reshape(ct, dec_seq, dec_batch, LANES)
        c, conv_state = _conv_module_sample(conv_hist_all, u4, conv_w, conv_b3, cg3, cbt3, l, conv_state, bb=32)
        c = c.reshape(rows_s, conv_ch)
        x1, = _out_proj(attn, c, xs, w_out_b, b_out3, g1, be1, l, 0, tm=rows_s, alpha=alpha)
        xs, su, sg = _conv_ffn(x1, ffn_hist_all, w_up_b, ffn_conv_w, fcb3, w_dn_b, g2, be2, l, 0, tm=rows_s,
                               tiles_per_seg=1, stride=dec_batch, st0=rows_s - st_rows, st_rows=st_rows,
                               alpha=alpha, hist_base=l, st_slabs=ffn_taps - 1, fc=2 * FF_CHUNK)
        sf.append(jnp.concatenate([su[..., :d_ff], sg[..., :d_ff]], axis=-1))
        if nxt:
            w_in_b, w_out_b, w_up_b, w_dn_b = (w[None] for w in cast)

    y_prompt = y_rows.reshape(batch, seq, d_model)
    y_sample = jnp.swapaxes(xs.reshape(dec_seq, dec_batch, d_model), 0, 1)
    kv_shape = (depth, dec_batch, WINDOW, N_KV_HEADS, HEAD_DIM)
    return (y_prompt, y_sample, jnp.stack(pk), jnp.stack(pv), jnp.stack(pc), jnp.stack(pf),
            kv_state[0].reshape(kv_shape), kv_state[1].reshape(kv_shape), conv_state, jnp.stack(sf))
```

```python
import functools

import jax
import jax.numpy as jnp
from jax.experimental import pallas as pl
from jax.experimental.pallas import tpu as pltpu

F32 = jnp.float32
BF16 = jnp.bfloat16

N_META = 16
HEAD_DIM = 64
N_KV_HEADS = 2
WINDOW = 128
LN_EPS = 1e-5
LANES = 128
SUBLANES = 8
BF16_ROWS = 16
FF_CHUNK = 512
VMEM_LIMIT = 56 * 1024 * 1024


def _round_up(x, m):
    return (x + m - 1) // m * m


def _layer_norm(y, g, b):
    mu = jnp.mean(y, axis=-1, keepdims=True)
    d = y - mu
    var = jnp.mean(d * d, axis=-1, keepdims=True)
    return d * jax.lax.rsqrt(var + LN_EPS) * g + b


def _params(*sem):
    return pltpu.CompilerParams(dimension_semantics=sem, vmem_limit_bytes=VMEM_LIMIT)


def _layer_spec(l, *block):
    zeros = (0,) * len(block)
    return pl.BlockSpec((None,) + block, lambda *_: (l,) + zeros)


class _SideCast:
    def __init__(self, src, layer, chunk_rows, n_steps, step_of):
        _, rows, cols = src.shape
        n_chunks = pl.cdiv(rows, chunk_rows)
        assert n_chunks <= n_steps and chunk_rows % BF16_ROWS == 0
        self.src = src
        chunk = lambda *g: jnp.minimum(step_of(*g), n_chunks - 1)
        self.in_spec = pl.BlockSpec((None, chunk_rows, cols), lambda *g: (layer, chunk(*g), 0))
        self.out_spec = pl.BlockSpec((chunk_rows, cols), lambda *g: (chunk(*g), 0))
        self.out_shape = jax.ShapeDtypeStruct((rows, cols), BF16)


def _run_side_casts(side_in, side_out):
    for src_ref, dst_ref in zip(side_in, side_out):
        dst_ref[...] = src_ref[...].astype(BF16)


def _in_proj_kernel(x_ref, w_ref, b_ref, q_ref, kv_ref, u_ref, *, attn_w, conv_ch, kv_w, q_scale):
    xb = x_ref[...].astype(BF16)

    def proj(c0, n):
        return jnp.dot(xb, w_ref[:, c0:c0 + n], preferred_element_type=F32) + b_ref[:, c0:c0 + n]

    q_ref[...] = (proj(0, attn_w) * q_scale).astype(BF16)
    kv_ref[...] = proj(attn_w, kv_w)
    a = proj(attn_w + kv_w, conv_ch)
    g = proj(attn_w + kv_w + conv_ch, conv_ch)
    u = a * jax.nn.sigmoid(g)
    for c in range(conv_ch // LANES):
        u_ref[c] = u[:, c * LANES:(c + 1) * LANES]


def _in_proj(x, w, b, l, wl, *, tm, attn_w, conv_ch, kv_w):
    rows, d = x.shape
    n = w.shape[2]
    ct = conv_ch // LANES
    kern = functools.partial(_in_proj_kernel, attn_w=attn_w, conv_ch=conv_ch, kv_w=kv_w,
                             q_scale=HEAD_DIM ** -0.5)
    return pl.pallas_call(
        kern,
        grid=(rows // tm,),
        in_specs=[pl.BlockSpec((tm, d), lambda i: (i, 0)),
                  pl.BlockSpec((None, d, n), lambda i: (wl, 0, 0), pipeline_mode=pl.Buffered(1)),
                  _layer_spec(l, 1, n)],
        out_specs=[pl.BlockSpec((tm, attn_w), lambda i: (i, 0)),
                   pl.BlockSpec((tm, kv_w), lambda i: (i, 0)),
                   pl.BlockSpec((ct, tm, LANES), lambda i: (0, i, 0))],
        out_shape=[jax.ShapeDtypeStruct((rows, attn_w), BF16),
                   jax.ShapeDtypeStruct((rows, kv_w), F32),
                   jax.ShapeDtypeStruct((ct, rows, LANES), F32)],
        compiler_params=_params("arbitrary"),
        name="in_proj",
    )(x, w, b)


def _swish_ln_tiles(tiles, g_ref, b_ref, ch):
    total = tiles[0]
    for tl in tiles[1:]:
        total = total + tl
    mu = jnp.sum(total, axis=-1, keepdims=True) / ch
    ds = [tl - mu for tl in tiles]
    sq = ds[0] * ds[0]
    for d in ds[1:]:
        sq = sq + d * d
    inv = jax.lax.rsqrt(jnp.sum(sq, axis=-1, keepdims=True) / ch + LN_EPS)
    out = []
    for c, d in enumerate(ds):
        y = d * inv * g_ref[:, c * LANES:(c + 1) * LANES] + b_ref[:, c * LANES:(c + 1) * LANES]
        out.append((y * jax.nn.sigmoid(y)).astype(BF16))
    return out


def _conv_kernel(hist_ref, u_ref, w_ref, cb_ref, g_ref, b_ref, *refs, tm, taps, qb, n_side):
    c_ref = refs[n_side]
    ext_ref, tail_ref, y_ref = refs[2 * n_side + 1:]
    _run_side_casts(refs[:n_side], refs[n_side + 1:2 * n_side + 1])
    ct = u_ref.shape[0]
    ch = ct * LANES
    nq = tm // SUBLANES
    nb = taps - 1
    hb = hist_ref.shape[2]
    t = pl.program_id(1)

    @pl.when(t == 0)
    def _():
        for c in range(ct):
            for e in range(nb):
                row = hist_ref[0, c, hb - nb + e:hb - nb + e + 1, :]
                tail_ref[c, e * SUBLANES:(e + 1) * SUBLANES, :] = jnp.broadcast_to(row, (SUBLANES, LANES))

    @pl.when(t > 0)
    def _():
        tail_ref[...] = ext_ref[:, nq * SUBLANES:(nq + nb) * SUBLANES, :]

    def load(q, carry):
        dst = pl.multiple_of((nb + q) * SUBLANES, SUBLANES)
        for c in range(ct):
            ext_ref[c, pl.ds(dst, SUBLANES), :] = u_ref[c, pl.ds(q, SUBLANES, stride=nq), :]
        return carry

    jax.lax.fori_loop(0, nq, load, 0)

    first_strip = jax.lax.broadcasted_iota(jnp.int32, (SUBLANES, LANES), 0) == 0
    for c in range(ct):
        for e in range(nb):
            cur = ext_ref[c, (nq + e) * SUBLANES:(nq + e + 1) * SUBLANES, :]
            prv = tail_ref[c, e * SUBLANES:(e + 1) * SUBLANES, :]
            ext_ref[c, e * SUBLANES:(e + 1) * SUBLANES, :] = jnp.where(
                first_strip, pltpu.roll(prv, 1, 0), pltpu.roll(cur, 1, 0))

    def conv_block(ib, carry):
        rows = qb * SUBLANES
        for c in range(ct):
            lanes = slice(c * LANES, (c + 1) * LANES)
            acc = w_ref[0:1, lanes] * ext_ref[c, pl.ds(pl.multiple_of(ib * rows, rows), rows), :]
            for j in range(1, taps):
                src = pl.multiple_of(ib * rows + j * SUBLANES, SUBLANES)
                acc = acc + w_ref[j:j + 1, lanes] * ext_ref[c, pl.ds(src, rows), :]
            y_ref[c, pl.ds(pl.multiple_of(ib * rows, rows), rows), :] = acc + cb_ref[:, lanes]
        return carry

    jax.lax.fori_loop(0, nq // qb, conv_block, 0)

    for r0 in range(0, tm, BF16_ROWS):
        tiles = []
        for c in range(ct):
            halves = []
            for rr in range(r0, r0 + BF16_ROWS, SUBLANES):
                s, q = divmod(rr, nq)
                halves.append(y_ref[c, pl.ds(q * SUBLANES + s, SUBLANES, stride=SUBLANES), :])
            tiles.append(jnp.concatenate(halves, axis=0))
        for c, o in enumerate(_swish_ln_tiles(tiles, g_ref, b_ref, ch)):
            c_ref[r0:r0 + BF16_ROWS, c * LANES:(c + 1) * LANES] = o


def _conv_module(hist, u, w, cb, g, b, l, *, tm, tiles_per_seg, side=()):
    ct, rows, _ = u.shape
    ch = ct * LANES
    n_seg, _, hb, _ = hist.shape
    taps = w.shape[1]
    nq = tm // SUBLANES
    qb = 8
    assert nq % qb == 0 and hb >= taps - 1 and nq >= taps - 1
    casts = [_SideCast(sw, sl, r, n_seg * tiles_per_seg, lambda s, t: s * tiles_per_seg + t) for sw, sl, r in side]
    kern = functools.partial(_conv_kernel, tm=tm, taps=taps, qb=qb, n_side=len(casts))
    return pl.pallas_call(
        kern,
        grid=(n_seg, tiles_per_seg),
        in_specs=[pl.BlockSpec((1, ct, hb, LANES), lambda s, t: (s, 0, 0, 0)),
                  pl.BlockSpec((ct, tm, LANES), lambda s, t: (0, s * tiles_per_seg + t, 0)),
                  _layer_spec(l, taps, ch), _layer_spec(l, 1, ch), _layer_spec(l, 1, ch), _layer_spec(l, 1, ch)]
        + [c_.in_spec for c_ in casts],
        out_specs=[pl.BlockSpec((tm, ch), lambda s, t: (s * tiles_per_seg + t, 0))] + [c_.out_spec for c_ in casts],
        out_shape=[jax.ShapeDtypeStruct((rows, ch), BF16)] + [c_.out_shape for c_ in casts],
        scratch_shapes=[pltpu.VMEM((ct, tm + (taps - 1) * SUBLANES, LANES), F32),
                        pltpu.VMEM((ct, (taps - 1) * SUBLANES, LANES), F32),
                        pltpu.VMEM((ct, tm, LANES), F32)],
        compiler_params=_params("arbitrary", "arbitrary"),
        name="conv_module",
    )(hist, u, w, cb, g, b, *[c_.src for c_ in casts])


def _conv_sample_kernel(hist_ref, u_ref, w_ref, cb_ref, g_ref, b_ref, *refs):
    c_ref, ns_ref = refs[-2:]
    n_hist = hist_ref.shape[0]
    ct, t_steps = u_ref.shape[0], u_ref.shape[1]
    for k in range(ns_ref.shape[0]):
        for s in range(n_hist):
            src = s + t_steps
            if src < n_hist:
                ns_ref[k, :, s, :] = hist_ref[src]
            else:
                for c in range(ct):
                    ns_ref[k, :, s, c * LANES:(c + 1) * LANES] = u_ref[c, src - n_hist]
    for t in range(t_steps):
        tiles = []
        for c in range(ct):
            lanes = slice(c * LANES, (c + 1) * LANES)
            acc = None
            for j in range(w_ref.shape[0]):
                s = t + j
                src = hist_ref[s, :, lanes] if s < n_hist else u_ref[c, s - n_hist]
                term = w_ref[j:j + 1, lanes] * src
                acc = term if acc is None else acc + term
            tiles.append(acc + cb_ref[:, lanes])
        for c, o in enumerate(_swish_ln_tiles(tiles, g_ref, b_ref, ct * LANES)):
            c_ref[t, :, c * LANES:(c + 1) * LANES] = o


def _state_out(bufs, l, depth, block, n_inputs, first_out):
    tail = (0,) * (len(block) - 1)
    if bufs is None:
        return pl.BlockSpec((depth,) + block, lambda i: (0, i) + tail), [], [], {}
    specs = [pl.BlockSpec(memory_space=pl.ANY)] * len(bufs)
    aliases = {n_inputs + k: first_out + k for k in range(len(bufs))}
    return pl.BlockSpec((1,) + block, lambda i: (l, i) + tail), list(bufs), specs, aliases


def _conv_module_sample(hist, u, w, cb, g, b, l, state_buf, *, bb):
    depth, n_hist, n_seq, ch = hist.shape
    ct, t_steps, _, _ = u.shape
    taps = w.shape[1]
    assert n_hist == taps - 1 and ch == ct * LANES
    st_spec, bufs, buf_specs, aliases = _state_out(None if state_buf is None else [state_buf], l, depth,
                                                   (bb, n_hist, ch), 6, 1)
    return pl.pallas_call(
        _conv_sample_kernel,
        grid=(n_seq // bb,),
        in_specs=[pl.BlockSpec((None, n_hist, bb, ch), lambda i: (l, 0, i, 0)),
                  pl.BlockSpec((ct, t_steps, bb, LANES), lambda i: (0, 0, i, 0)),
                  _layer_spec(l, taps, ch), _layer_spec(l, 1, ch), _layer_spec(l, 1, ch), _layer_spec(l, 1, ch)]
        + buf_specs,
        out_specs=[pl.BlockSpec((t_steps, bb, ch), lambda i: (0, i, 0)), st_spec],
        out_shape=[jax.ShapeDtypeStruct((t_steps, n_seq, ch), BF16),
                   jax.ShapeDtypeStruct((depth, n_seq, n_hist, ch), F32)],
        input_output_aliases=aliases,
        compiler_params=_params("arbitrary"),
        name="conv_module_sample",
    )(hist, u, w, cb, g, b, *bufs)


def _attn_prompt_kernel(sink_ref, q_ref, kvp_ref, kvc_ref, *refs, n_heads, group):
    n_side = (len(refs) - 1) // 2
    o_ref = refs[n_side]
    _run_side_casts(refs[:n_side], refs[n_side + 1:])
    blk = q_ref.shape[0]
    i = pl.program_id(1)
    kv = jnp.concatenate([kvp_ref[...], kvc_ref[...]], axis=0)
    kband = kv[:, 0:LANES]
    vband = kv[:, LANES:2 * LANES]
    kroll = pltpu.roll(kband, HEAD_DIM, 1)
    vroll = pltpu.roll(vband, HEAD_DIM, 1)
    lane = jax.lax.broadcasted_iota(jnp.int32, (2 * blk, LANES), 1)
    lo = lane < HEAD_DIM

    def lo_hi(x_lo, x_hi):
        return jnp.concatenate([jnp.where(lo, x_lo, 0.0), jnp.where(lo, 0.0, x_hi)], axis=0).astype(BF16)

    kab = [lo_hi(kband, kroll), lo_hi(kroll, kband)]
    vab = [lo_hi(vband, vroll), lo_hi(vroll, vband)]

    r = jax.lax.broadcasted_iota(jnp.int32, (blk, 2 * blk), 0)
    c = jax.lax.broadcasted_iota(jnp.int32, (blk, 2 * blk), 1)
    mask = (c >= r) & (c <= r + WINDOW) & ((c >= blk) | (i > 0))
    out_lo = jax.lax.broadcasted_iota(jnp.int32, (blk, LANES), 1) < HEAD_DIM

    pairs_per_kv = group // 2
    for kvh in range(n_heads // group):
        pairs = range(kvh * pairs_per_kv, (kvh + 1) * pairs_per_kv)
        q4 = jnp.concatenate([q_ref[:, p * LANES:(p + 1) * LANES] for p in pairs], axis=0)
        s4 = jax.lax.dot_general(q4, kab[kvh], (((1,), (1,)), ((), ())), preferred_element_type=F32)
        p_rows, inv_rows = [], []
        for n, pair in enumerate(pairs):
            ps, invs = [], []
            for half in range(2):
                sink = sink_ref[2 * pair + half]
                s = jnp.where(mask, s4[n * blk:(n + 1) * blk, half * 2 * blk:(half + 1) * 2 * blk], -jnp.inf)
                m = jnp.maximum(jnp.max(s, axis=-1, keepdims=True), sink)
                p = jnp.exp(s - m)
                denom = jnp.sum(p, axis=-1, keepdims=True) + jnp.exp(sink - m)
                ps.append(p.astype(BF16))
                invs.append(1.0 / denom)
            p_rows.append(jnp.concatenate(ps, axis=1))
            inv_rows.append(jnp.where(out_lo, invs[0], invs[1]))
        o4 = jnp.dot(jnp.concatenate(p_rows, axis=0), vab[kvh], preferred_element_type=F32)
        for n, pair in enumerate(pairs):
            o_ref[:, pair * LANES:(pair + 1) * LANES] = (o4[n * blk:(n + 1) * blk] * inv_rows[n]).astype(BF16)


def _attn_prompt(sinks, q, kv, *, n_seg, blocks_per_seg, blk, side=()):
    rows, attn_w = q.shape
    casts = [_SideCast(w, wl, r, n_seg * blocks_per_seg, lambda s, i: s * blocks_per_seg + i) for w, wl, r in side]
    kv_w = kv.shape[1]
    n_heads = attn_w // HEAD_DIM
    kern = functools.partial(_attn_prompt_kernel, n_heads=n_heads, group=n_heads // N_KV_HEADS)
    return pl.pallas_call(
        kern,
        grid=(n_seg, blocks_per_seg),
        in_specs=[pl.BlockSpec(memory_space=pltpu.SMEM),
                  pl.BlockSpec((blk, attn_w), lambda s, i: (s * blocks_per_seg + i, 0)),
                  pl.BlockSpec((blk, kv_w), lambda s, i: (jnp.maximum(s * blocks_per_seg + i - 1, 0), 0)),
                  pl.BlockSpec((blk, kv_w), lambda s, i: (s * blocks_per_seg + i, 0))]
        + [c.in_spec for c in casts],
        out_specs=[pl.BlockSpec((blk, attn_w), lambda s, i: (s * blocks_per_seg + i, 0))]
        + [c.out_spec for c in casts],
        out_shape=[jax.ShapeDtypeStruct((rows, attn_w), BF16)] + [c.out_shape for c in casts],
        compiler_params=_params("arbitrary", "arbitrary"),
        name="attn_prompt",
    )(sinks, q, kv, kv, *[c.src for c in casts])


def _attn_sample_kernel(sink_ref, q_ref, hk_ref, hv_ref, nkv_ref, *refs, t_steps, group):
    o_ref, nk_ref, nv_ref = refs[-3:]
    bb, nq, _ = q_ref.shape
    npad = nkv_ref.shape[1]
    for st_ref, h_ref, lane0 in ((nk_ref, hk_ref, 0), (nv_ref, hv_ref, LANES)):
        for k in range(st_ref.shape[0]):
            st_ref[k, :, 0:WINDOW - t_steps, :] = h_ref[:, t_steps:WINDOW, :]
            st_ref[k, :, WINDOW - t_steps:WINDOW, :] = nkv_ref[:, 0:t_steps, lane0:lane0 + LANES]
    q = q_ref[...]
    hk = hk_ref[...].astype(BF16)
    hv = hv_ref[...].astype(BF16)
    nk = nkv_ref[:, :, 0:LANES].astype(BF16)
    nv = nkv_ref[:, :, LANES:2 * LANES].astype(BF16)
    s_h = jnp.einsum('bqd,bkd->bqk', q, hk, preferred_element_type=F32)
    s_n = jnp.einsum('bqd,bkd->bqk', q, nk, preferred_element_type=F32)
    row = jax.lax.broadcasted_iota(jnp.int32, (1, nq, 1), 1)
    t = (row // group) % t_steps
    kvh = row // (group * t_steps)
    head = kvh * group + row % group
    sink = jnp.zeros((1, nq, 1), F32)
    for h in range(N_KV_HEADS * group):
        sink = jnp.where(head == h, sink_ref[h], sink)
    j_h = jax.lax.broadcasted_iota(jnp.int32, (1, nq, WINDOW), 2)
    j_n = jax.lax.broadcasted_iota(jnp.int32, (1, nq, npad), 2)
    s_h = jnp.where(j_h >= t, s_h, -jnp.inf)
    s_n = jnp.where(j_n <= t, s_n, -jnp.inf)
    m = jnp.maximum(jnp.maximum(jnp.max(s_h, axis=-1, keepdims=True), jnp.max(s_n, axis=-1, keepdims=True)), sink)
    p_h = jnp.exp(s_h - m)
    p_n = jnp.exp(s_n - m)
    denom = jnp.sum(p_h, axis=-1, keepdims=True) + jnp.sum(p_n, axis=-1, keepdims=True) + jnp.exp(sink - m)
    o = (jnp.einsum('bqk,bkd->bqd', p_h.astype(BF16), hv, preferred_element_type=F32)
         + jnp.einsum('bqk,bkd->bqd', p_n.astype(BF16), nv, preferred_element_type=F32))
    lane_kvh = jax.lax.broadcasted_iota(jnp.int32, (1, nq, LANES), 2) // HEAD_DIM
    o_ref[...] = jnp.where(lane_kvh == kvh, o / denom, 0.0)


def _attn_sample(sinks, qpad, hist_k, hist_v, new_kv, l, state_bufs, *, t_steps, bb):
    n_seq, nq, _ = qpad.shape
    npad = new_kv.shape[1]
    n_heads = sinks.shape[0]
    kern = functools.partial(_attn_sample_kernel, t_steps=t_steps, group=n_heads // N_KV_HEADS)
    st_spec, bufs, buf_specs, aliases = _state_out(state_bufs, l, hist_k.shape[0], (bb, WINDOW, LANES), 5, 1)
    st_shape = jax.ShapeDtypeStruct(hist_k.shape, F32)
    return pl.pallas_call(
        kern,
        grid=(n_seq // bb,),
        in_specs=[pl.BlockSpec(memory_space=pltpu.SMEM),
                  pl.BlockSpec((bb, nq, LANES), lambda i: (i, 0, 0)),
                  pl.BlockSpec((None, bb, WINDOW, LANES), lambda i: (l, i, 0, 0)),
                  pl.BlockSpec((None, bb, WINDOW, LANES), lambda i: (l, i, 0, 0)),
                  pl.BlockSpec((bb, npad, 2 * LANES), lambda i: (i, 0, 0))] + buf_specs,
        out_specs=[pl.BlockSpec((bb, nq, LANES), lambda i: (i, 0, 0)), st_spec, st_spec],
        out_shape=[jax.ShapeDtypeStruct((n_seq, nq, LANES), F32), st_shape, st_shape],
        input_output_aliases=aliases,
        compiler_params=_params("arbitrary"),
        name="attn_sample",
    )(sinks, qpad, hist_k, hist_v, new_kv, *bufs)


def _out_proj_kernel(a_ref, c_ref, x_ref, w_ref, b_ref, g_ref, beta_ref, *refs, alpha):
    n_side = (len(refs) - 1) // 2
    o_ref = refs[n_side]
    _run_side_casts(refs[:n_side], refs[n_side + 1:])
    aw = a_ref.shape[1]
    hm = a_ref.shape[0] // 2
    for h in range(2):
        rows = slice(h * hm, (h + 1) * hm)
        mix = (jnp.dot(a_ref[rows, :], w_ref[0:aw, :], preferred_element_type=F32)
               + jnp.dot(c_ref[rows, :], w_ref[aw:, :], preferred_element_type=F32) + b_ref[...])
        o_ref[rows, :] = _layer_norm(alpha * x_ref[rows, :] + mix, g_ref[...], beta_ref[...])


def _out_proj(attn, c, x, w, b, g, beta, l, wl, *, tm, alpha, side=()):
    rows, d = x.shape
    aw, cw = attn.shape[1], c.shape[1]
    casts = [_SideCast(sw, wl, r, rows // tm, lambda i: i) for sw, wl, r in side]
    return pl.pallas_call(
        functools.partial(_out_proj_kernel, alpha=alpha),
        grid=(rows // tm,),
        in_specs=[pl.BlockSpec((tm, aw), lambda i: (i, 0)),
                  pl.BlockSpec((tm, cw), lambda i: (i, 0)),
                  pl.BlockSpec((tm, d), lambda i: (i, 0)),
                  pl.BlockSpec((None, aw + cw, d), lambda i: (wl, 0, 0), pipeline_mode=pl.Buffered(1)),
                  _layer_spec(l, 1, d), _layer_spec(l, 1, d), _layer_spec(l, 1, d)] + [c_.in_spec for c_ in casts],
        out_specs=[pl.BlockSpec((tm, d), lambda i: (i, 0))] + [c_.out_spec for c_ in casts],
        out_shape=[jax.ShapeDtypeStruct((rows, d), F32)] + [c_.out_shape for c_ in casts],
        compiler_params=_params("arbitrary"),
        name="out_proj",
    )(attn, c, x, w, b, g, beta, *[c_.src for c_ in casts])


def _ffn_kernel(x_ref, hu_ref, hg_ref, wu_ref, wg_ref, cwu_ref, cwg_ref, cbu_ref, cbg_ref, wdn_ref, g_ref, beta_ref,
                *refs, tm, prev, stride, tiles_per_seg, st0, st_rows, st_slabs, alpha, rb, n_split, dup_cols, n_side,
                halo):
    side_in, refs = refs[:n_side], refs[n_side:]
    o_ref, su_ref, sg_ref = refs[:3]
    side_out, refs = refs[3:3 + n_side], refs[3 + n_side:]
    xb_ref, hsu_ref, hsg_ref, p_ref = refs[:4]
    carry = refs[4:]
    i = pl.program_id(0)
    j = pl.program_id(1)
    nj = pl.num_programs(1)
    fc = p_ref.shape[1]
    hm = tm // n_split

    @pl.when(j == 0)
    def _():
        xb_ref[...] = x_ref[...].astype(BF16)
        o_ref[...] = jnp.zeros_like(o_ref)
        if carry:
            @pl.when(i == 0)
            def _():
                for cr in carry:
                    cr[...] = jnp.zeros_like(cr)

    for h in range(n_split):
        if halo:
            lo, hi = (0 if h == 0 else prev + h * hm), prev + (h + 1) * hm
            xs = xb_ref[lo:hi, :]
        else:
            lo, hi = prev + h * hm, prev + (h + 1) * hm
            xs = xb_ref[h * hm:(h + 1) * hm, :]
        hsu_ref[lo:hi, :] = jnp.dot(xs, wu_ref[0], preferred_element_type=F32)
        hsg_ref[lo:hi, :] = jnp.dot(xs, wg_ref[0], preferred_element_type=F32)
    _run_side_casts(side_in, side_out)

    if halo:
        pass
    elif carry:
        first = (i % tiles_per_seg) == 0
        for hs_ref, hist_ref, cr in ((hsu_ref, hu_ref, carry[0]), (hsg_ref, hg_ref, carry[1])):
            hs_ref[0:prev, :] = jnp.where(first, hist_ref[0], cr[j])
            cr[j] = hs_ref[tm:tm + prev, :]
    else:
        hsu_ref[0:prev, :] = hu_ref[0]
        hsg_ref[0:prev, :] = hg_ref[0]
    for st_ref, hs_ref in ((su_ref, hsu_ref), (sg_ref, hsg_ref)):
        v = hs_ref[prev + st0:prev + st0 + st_rows, :]
        if dup_cols:
            shifted = jnp.concatenate([v[:, dup_cols:], jnp.zeros((st_rows, dup_cols), F32)], axis=1)
            v = jnp.where(j == nj - 1, shifted, v)
        if st_slabs:
            nb = st_rows // st_slabs
            for s in range(st_slabs):
                st_ref[:, s, :] = v[s * nb:(s + 1) * nb, :]
        else:
            st_ref[0] = v

    def conv(hs_ref, cw_ref, cb_ref, r0):
        return (cw_ref[0, 2:3, :] * hs_ref[prev + r0:prev + r0 + rb, :]
                + cw_ref[0, 1:2, :] * hs_ref[prev - stride + r0:prev - stride + r0 + rb, :]
                + cw_ref[0, 0:1, :] * hs_ref[prev - 2 * stride + r0:prev - 2 * stride + r0 + rb, :]
                + cb_ref[0])

    col = jax.lax.broadcasted_iota(jnp.int32, (rb, fc), 1)
    keep = (col >= dup_cols) | (j < nj - 1)
    for h in range(n_split):
        for r0 in range(h * hm, (h + 1) * hm, rb):
            yu = conv(hsu_ref, cwu_ref, cbu_ref, r0)
            yg = conv(hsg_ref, cwg_ref, cbg_ref, r0)
            p_ref[r0:r0 + rb, :] = jnp.where(keep, yg * jax.nn.sigmoid(yg) * yu, 0.0).astype(BF16)
        rows = slice(h * hm, (h + 1) * hm)
        o_ref[rows, :] += jnp.dot(p_ref[rows, :], wdn_ref[0], preferred_element_type=F32)

    @pl.when(j == nj - 1)
    def _():
        x_rows = x_ref[prev:prev + tm, :] if halo else x_ref[...]
        o_ref[...] = _layer_norm(alpha * x_rows + o_ref[...], g_ref[...], beta_ref[...])


def _conv_ffn(x, hist, wup, cw, cb, wdn, g, beta, l, wl, *, tm, tiles_per_seg, stride, st0, st_rows, alpha, hist_base=0,
              st_slabs=0, side=(), seg_rows=None, single_buffer_rows=False):
    halo = seg_rows is not None
    d = x.shape[1]
    _, prev, _ = hist.shape
    d_ff = wdn.shape[1]
    fc = FF_CHUNK
    nblk = d_ff // LANES
    cblk = fc // LANES
    nj = pl.cdiv(d_ff, fc)
    dup_cols = nj * fc - d_ff
    n_tiles = x.shape[0] // seg_rows * tiles_per_seg if halo else x.shape[0] // tm
    rows = n_tiles * tm
    n_split = 2
    hm = tm // n_split
    rb = 32 if hm % 32 == 0 else 48
    assert d_ff % LANES == 0 and tm % n_split == 0 and hm % rb == 0 and nblk >= cblk
    row_mode = {"pipeline_mode": pl.Buffered(1)} if single_buffer_rows else {}

    def blk0(j):
        return jnp.minimum(j * cblk, nblk - cblk)

    el = pl.Element
    col_u = lambda j: LANES * blk0(j)
    col_g = lambda j: LANES * (nblk + blk0(j))
    casts = [_SideCast(sw, wl, r, n_tiles * nj, lambda i, j: i * nj + j) for sw, wl, r in side]
    kern = functools.partial(_ffn_kernel, tm=tm, prev=prev, stride=stride, tiles_per_seg=tiles_per_seg, st0=st0,
                             st_rows=st_rows, st_slabs=st_slabs, alpha=alpha, rb=rb, n_split=n_split,
                             dup_cols=dup_cols, n_side=len(casts), halo=halo)
    x_rows = tm + prev if halo else tm
    scratch = [pltpu.VMEM((x_rows, d), BF16), pltpu.VMEM((prev + tm, fc), F32), pltpu.VMEM((prev + tm, fc), F32),
               pltpu.VMEM((tm, fc), BF16)]
    if tiles_per_seg > 1 and not halo:
        scratch += [pltpu.VMEM((nj, prev, fc), F32), pltpu.VMEM((nj, prev, fc), F32)]
    if halo:
        assert prev % BF16_ROWS == 0 and seg_rows % BF16_ROWS == 0 and tm % BF16_ROWS == 0
        assert prev + tiles_per_seg * tm <= seg_rows
        x_spec = pl.BlockSpec((el(x_rows), el(d)), lambda i, j: (
            BF16_ROWS * ((i // tiles_per_seg) * (seg_rows // BF16_ROWS) + (i % tiles_per_seg) * (tm // BF16_ROWS)), 0),
            **row_mode)
    else:
        x_spec = pl.BlockSpec((tm, d), lambda i, j: (i, 0), **row_mode)
    hist_blk = (el(1), el(prev), el(fc))
    if st_slabs:
        assert n_tiles == 1 and st_rows % st_slabs == 0
        st_dims = (st_rows // st_slabs, st_slabs)
        st_spec = pl.BlockSpec(st_dims + (fc,), lambda i, j: (0, 0, j))
    else:
        st_dims = (n_tiles, st_rows)
        st_spec = pl.BlockSpec((1, st_rows, fc), lambda i, j: (i, 0, j))
    return pl.pallas_call(
        kern,
        grid=(n_tiles, nj),
        in_specs=[x_spec,
                  pl.BlockSpec(hist_blk, lambda i, j: (hist_base + i // tiles_per_seg, 0, col_u(j))),
                  pl.BlockSpec(hist_blk, lambda i, j: (hist_base + i // tiles_per_seg, 0, col_g(j))),
                  pl.BlockSpec((el(1), el(d), el(fc)), lambda i, j: (wl, 0, col_u(j))),
                  pl.BlockSpec((el(1), el(d), el(fc)), lambda i, j: (wl, 0, col_g(j))),
                  pl.BlockSpec((el(1), el(3), el(fc)), lambda i, j: (l, 0, col_u(j))),
                  pl.BlockSpec((el(1), el(3), el(fc)), lambda i, j: (l, 0, col_g(j))),
                  pl.BlockSpec((el(1), el(1), el(fc)), lambda i, j: (l, 0, col_u(j))),
                  pl.BlockSpec((el(1), el(1), el(fc)), lambda i, j: (l, 0, col_g(j))),
                  pl.BlockSpec((el(1), el(fc), el(d)), lambda i, j: (wl, col_u(j), 0)),
                  _layer_spec(l, 1, d), _layer_spec(l, 1, d)] + [c_.in_spec for c_ in casts],
        out_specs=[pl.BlockSpec((tm, d), lambda i, j: (i, 0), **row_mode), st_spec, st_spec]
        + [c_.out_spec for c_ in casts],
        out_shape=[jax.ShapeDtypeStruct((rows, d), F32),
                   jax.ShapeDtypeStruct(st_dims + (nj * fc,), F32),
                   jax.ShapeDtypeStruct(st_dims + (nj * fc,), F32)] + [c_.out_shape for c_ in casts],
        scratch_shapes=scratch,
        compiler_params=_params("arbitrary", "arbitrary"),
        name="conv_ffn",
    )(x, hist, hist, wup, wup, cw, cw, cb, cb, wdn, g, beta, *[c_.src for c_ in casts])


def kernel(x_prompt, x_sample, state_attn_k, state_attn_v, state_conv, state_ffn_conv, meta_tokens, w_in, b_in, attn_sinks, conv_w, conv_b, conv_ln_g, conv_ln_b, w_out, b_out, ln1_g, ln1_b, ffn_w_up, ffn_conv_w, ffn_conv_b, ffn_w_down, ln2_g, ln2_b):
    batch, seq, d_model = x_prompt.shape
    dec_batch, dec_seq, _ = x_sample.shape
    depth = w_in.shape[0]
    conv_ch = conv_w.shape[2]
    conv_taps = conv_w.shape[1]
    ffn_taps = ffn_conv_w.shape[1]
    attn_w = d_model - conv_ch
    kv_w = 2 * N_KV_HEADS * HEAD_DIM
    n_heads = attn_w // HEAD_DIM
    group = n_heads // N_KV_HEADS
    d_ff = ffn_w_down.shape[1]
    ct = conv_ch // LANES
    alpha = (2 * depth) ** 0.25
    assert ffn_taps == 3 and kv_w == 2 * LANES and dec_batch % SUBLANES == 0
    assert ffn_taps - 1 <= dec_seq <= min(conv_taps - 1, SUBLANES) and seq >= WINDOW >= conv_taps

    seq_all = N_META + seq
    lp = _round_up(seq_all, WINDOW)
    tiles_p = 6
    tm_p = lp // tiles_p
    assert tm_p % 64 == 0
    meta = jnp.broadcast_to(meta_tokens[None].astype(x_prompt.dtype), (batch, N_META, d_model))
    xp = jnp.concatenate([meta, x_prompt, jnp.zeros((batch, lp - seq_all, d_model), x_prompt.dtype)], axis=1)
    xp = xp.reshape(batch * lp, d_model)
    rows_s = dec_seq * dec_batch
    xs = jnp.swapaxes(x_sample, 0, 1).reshape(rows_s, d_model)

    conv_hist_rows = 32
    ffn_prev_p = SUBLANES
    last_p = seq_all - 1
    ffn_state_tile = last_p // tm_p
    ffn_st0 = (last_p % tm_p) // SUBLANES * SUBLANES
    assert (last_p - 1) // tm_p == ffn_state_tile and (last_p - 1) % tm_p >= ffn_st0

    rows3 = lambda v: v[:, None, :]
    b_in3, conv_b3, cg3, cbt3 = rows3(b_in), rows3(conv_b), rows3(conv_ln_g), rows3(conv_ln_b)
    b_out3, g1, be1, g2, be2 = rows3(b_out), rows3(ln1_g), rows3(ln1_b), rows3(ln2_g), rows3(ln2_b)
    fcb3 = rows3(ffn_conv_b)
    zero_conv_hist = jnp.zeros((batch, ct, conv_hist_rows, LANES), F32)
    zero_ffn_hist = jnp.zeros((batch, ffn_prev_p, 2 * d_ff), F32)
    tm_y = 1024
    tiles_y = seq // tm_y
    assert seq % tm_y == 0 and N_META % BF16_ROWS == 0
    zero_ffn_halo = jnp.zeros((batch, N_META, 2 * d_ff), F32)
    hk_all = state_attn_k.reshape(depth, dec_batch, WINDOW, LANES)
    hv_all = state_attn_v.reshape(depth, dec_batch, WINDOW, LANES)
    conv_hist_all = jnp.swapaxes(state_conv, 1, 2)
    st_rows = (ffn_taps - 1) * dec_batch
    ffn_hist_all = jnp.swapaxes(state_ffn_conv, 1, 2).reshape(depth, st_rows, 2 * d_ff)

    w_in_b = w_in[:1].astype(BF16)
    w_out_b = w_out[:1].astype(BF16)
    w_up_b = w_dn_b = None

    pk, pv, pc, pf, sf = [], [], [], [], []
    kv_state = conv_state = None
    for l in range(depth):
        sinks = attn_sinks[l]
        first = l == 0
        nxt = l + 1 < depth

        q, kv, u = _in_proj(xp, w_in_b, b_in3, l, 0, tm=tm_p, attn_w=attn_w, conv_ch=conv_ch, kv_w=kv_w)
        attn, = _attn_prompt(sinks, q, kv, n_seg=batch, blocks_per_seg=lp // WINDOW, blk=WINDOW)
        c, *cast = _conv_module(zero_conv_hist, u, conv_w, conv_b3, cg3, cbt3, l, tm=tm_p, tiles_per_seg=tiles_p,
                                side=[(ffn_w_up, 0, 176)] if first else [])
        if first:
            w_up_b = cast[0][None]
        x1, *cast = _out_proj(attn, c, xp, w_out_b, b_out3, g1, be1, l, 0, tm=tm_p, alpha=alpha,
                              side=[(ffn_w_down, 0, 512)] if first else [])
        if first:
            w_dn_b = cast[0][None]
        if nxt:
            next_w = [(w_in, l + 1, 16), (w_out, l + 1, 16), (ffn_w_up, l + 1, 16), (ffn_w_down, l + 1, 48)]
            xp, su, sg, *cast = _conv_ffn(x1, zero_ffn_hist, w_up_b, ffn_conv_w, fcb3, w_dn_b, g2, be2, l, 0, tm=tm_p,
                                          tiles_per_seg=tiles_p, stride=1, st0=ffn_st0, st_rows=SUBLANES,
                                          alpha=alpha, side=next_w)
            st_tile, st_off, st_tiles = ffn_state_tile, last_p % tm_p - ffn_st0 - (ffn_taps - 2), tiles_p
        else:
            y_rows, su, sg = _conv_ffn(x1, zero_ffn_halo, w_up_b, ffn_conv_w, fcb3, w_dn_b, g2, be2, l, 0, tm=tm_y,
                                       tiles_per_seg=tiles_y, stride=1, st0=tm_y - SUBLANES, st_rows=SUBLANES,
                                       alpha=alpha, seg_rows=lp, single_buffer_rows=True)
            st_tile, st_off, st_tiles = tiles_y - 1, SUBLANES - (ffn_taps - 1), tiles_y
        kv3 = kv.reshape(batch, lp, kv_w)[:, seq_all - WINDOW:seq_all]
        pk.append(kv3[..., :kv_w // 2].reshape(batch, WINDOW, N_KV_HEADS, HEAD_DIM))
        pv.append(kv3[..., kv_w // 2:].reshape(batch, WINDOW, N_KV_HEADS, HEAD_DIM))
        u_tail = u.reshape(ct, batch, lp, LANES)[:, :, seq_all - (conv_taps - 1):seq_all]
        pc.append(u_tail.transpose(1, 2, 0, 3).reshape(batch, conv_taps - 1, conv_ch))
        hst = jnp.concatenate([su[..., :d_ff], sg[..., :d_ff]], axis=-1).reshape(batch, st_tiles, SUBLANES, 2 * d_ff)
        pf.append(hst[:, st_tile, st_off:st_off + ffn_taps - 1])

        q, kv, u = _in_proj(xs, w_in_b, b_in3, l, 0, tm=rows_s, attn_w=attn_w, conv_ch=conv_ch, kv_w=kv_w)
        q5 = q.reshape(dec_seq, dec_batch, N_KV_HEADS, group, HEAD_DIM).transpose(1, 2, 0, 3, 4)
        q5 = q5.reshape(dec_batch, N_KV_HEADS, dec_seq * group, HEAD_DIM)
        zq = jnp.zeros_like(q5[:, 0])
        qpad = jnp.concatenate([jnp.concatenate([q5[:, 0], zq], axis=-1),
                                jnp.concatenate([zq, q5[:, 1]], axis=-1)], axis=1)
        kv_new = kv.reshape(dec_seq, dec_batch, kv_w).transpose(1, 0, 2)
        kv_new_pad = jnp.pad(kv_new, ((0, 0), (0, SUBLANES - dec_seq), (0, 0)))
        o, *kv_state = _attn_sample(sinks, qpad, hk_all, hv_all, kv_new_pad, l, kv_state, t_steps=dec_seq, bb=8)
        o = (o[..., :HEAD_DIM] + o[..., HEAD_DIM:]).reshape(dec_batch, N_KV_HEADS, dec_seq, group, HEAD_DIM)
        attn = o.transpose(2, 0, 1, 3, 4).reshape(rows_s, attn_w).astype(BF16)
        u4 = u.reshape(ct, dec_seq, dec_batch, LANES)
        c, conv_state = _conv_module_sample(conv_hist_all, u4, conv_w, conv_b3, cg3, cbt3, l, conv_state, bb=32)
        c = c.reshape(rows_s, conv_ch)
        x1, = _out_proj(attn, c, xs, w_out_b, b_out3, g1, be1, l, 0, tm=rows_s, alpha=alpha)
        xs, su, sg = _conv_ffn(x1, ffn_hist_all, w_up_b, ffn_conv_w, fcb3, w_dn_b, g2, be2, l, 0, tm=rows_s,
                               tiles_per_seg=1, stride=dec_batch, st0=rows_s - st_rows, st_rows=st_rows,
                               alpha=alpha, hist_base=l, st_slabs=ffn_taps - 1)
        sf.append(jnp.concatenate([su[..., :d_ff], sg[..., :d_ff]], axis=-1))
        if nxt:
            w_in_b, w_out_b, w_up_b, w_dn_b = (w[None] for w in cast)

    y_prompt = y_rows.reshape(batch, seq, d_model)
    y_sample = jnp.swapaxes(xs.reshape(dec_seq, dec_batch, d_model), 0, 1)
    kv_shape = (depth, dec_batch, WINDOW, N_KV_HEADS, HEAD_DIM)
    return (y_prompt, y_sample, jnp.stack(pk), jnp.stack(pv), jnp.stack(pc), jnp.stack(pf),
            kv_state[0].reshape(kv_shape), kv_state[1].reshape(kv_shape), conv_state, jnp.stack(sf))
```

```python
import functools

import jax
import jax.numpy as jnp
from jax.experimental import pallas as pl
from jax.experimental.pallas import tpu as pltpu

F32 = jnp.float32
BF16 = jnp.bfloat16

N_META = 16
HEAD_DIM = 64
N_KV_HEADS = 2
WINDOW = 128
LN_EPS = 1e-5
LANES = 128
SUBLANES = 8
BF16_ROWS = 16
FF_CHUNK = 512
VMEM_LIMIT = 56 * 1024 * 1024
FFN_VMEM_LIMIT = 60 * 1024 * 1024


def _round_up(x, m):
    return (x + m - 1) // m * m


def _layer_norm(y, g, b):
    mu = jnp.mean(y, axis=-1, keepdims=True)
    d = y - mu
    var = jnp.mean(d * d, axis=-1, keepdims=True)
    return d * jax.lax.rsqrt(var + LN_EPS) * g + b


def _params(*sem, vmem_limit=VMEM_LIMIT):
    return pltpu.CompilerParams(dimension_semantics=sem, vmem_limit_bytes=vmem_limit)


def _layer_spec(l, *block):
    zeros = (0,) * len(block)
    return pl.BlockSpec((None,) + block, lambda *_: (l,) + zeros)


class _SideCast:
    def __init__(self, src, layer, chunk_rows, n_steps, step_of):
        _, rows, cols = src.shape
        n_chunks = pl.cdiv(rows, chunk_rows)
        assert n_chunks <= n_steps and chunk_rows % BF16_ROWS == 0
        self.src = src
        chunk = lambda *g: jnp.minimum(step_of(*g), n_chunks - 1)
        self.in_spec = pl.BlockSpec((None, chunk_rows, cols), lambda *g: (layer, chunk(*g), 0))
        self.out_spec = pl.BlockSpec((chunk_rows, cols), lambda *g: (chunk(*g), 0))
        self.out_shape = jax.ShapeDtypeStruct((rows, cols), BF16)


def _run_side_casts(side_in, side_out):
    for src_ref, dst_ref in zip(side_in, side_out):
        dst_ref[...] = src_ref[...].astype(BF16)


def _in_proj_kernel(x_ref, w_ref, b_ref, q_ref, kv_ref, u_ref, *, attn_w, conv_ch, kv_w, q_scale):
    xb = x_ref[...].astype(BF16)

    def proj(c0, n):
        return jnp.dot(xb, w_ref[:, c0:c0 + n], preferred_element_type=F32) + b_ref[:, c0:c0 + n]

    q_ref[...] = (proj(0, attn_w) * q_scale).astype(BF16)
    kv_ref[...] = proj(attn_w, kv_w)
    a = proj(attn_w + kv_w, conv_ch)
    g = proj(attn_w + kv_w + conv_ch, conv_ch)
    u = a * jax.nn.sigmoid(g)
    for c in range(conv_ch // LANES):
        u_ref[c] = u[:, c * LANES:(c + 1) * LANES]


def _in_proj(x, w, b, l, wl, *, tm, attn_w, conv_ch, kv_w):
    rows, d = x.shape
    n = w.shape[2]
    ct = conv_ch // LANES
    kern = functools.partial(_in_proj_kernel, attn_w=attn_w, conv_ch=conv_ch, kv_w=kv_w,
                             q_scale=HEAD_DIM ** -0.5)
    return pl.pallas_call(
        kern,
        grid=(rows // tm,),
        in_specs=[pl.BlockSpec((tm, d), lambda i: (i, 0)),
                  pl.BlockSpec((None, d, n), lambda i: (wl, 0, 0), pipeline_mode=pl.Buffered(1)),
                  _layer_spec(l, 1, n)],
        out_specs=[pl.BlockSpec((tm, attn_w), lambda i: (i, 0)),
                   pl.BlockSpec((tm, kv_w), lambda i: (i, 0)),
                   pl.BlockSpec((ct, tm, LANES), lambda i: (0, i, 0))],
        out_shape=[jax.ShapeDtypeStruct((rows, attn_w), BF16),
                   jax.ShapeDtypeStruct((rows, kv_w), F32),
                   jax.ShapeDtypeStruct((ct, rows, LANES), F32)],
        compiler_params=_params("arbitrary"),
        name="in_proj",
    )(x, w, b)


def _swish_ln_tiles(tiles, g_ref, b_ref, ch):
    total = tiles[0]
    for tl in tiles[1:]:
        total = total + tl
    mu = jnp.sum(total, axis=-1, keepdims=True) / ch
    ds = [tl - mu for tl in tiles]
    sq = ds[0] * ds[0]
    for d in ds[1:]:
        sq = sq + d * d
    inv = jax.lax.rsqrt(jnp.sum(sq, axis=-1, keepdims=True) / ch + LN_EPS)
    out = []
    for c, d in enumerate(ds):
        y = d * inv * g_ref[:, c * LANES:(c + 1) * LANES] + b_ref[:, c * LANES:(c + 1) * LANES]
        out.append((y * jax.nn.sigmoid(y)).astype(BF16))
    return out


def _conv_kernel(hist_ref, u_ref, w_ref, cb_ref, g_ref, b_ref, *refs, tm, taps, qb, n_side):
    c_ref = refs[n_side]
    ext_ref, tail_ref, y_ref = refs[2 * n_side + 1:]
    _run_side_casts(refs[:n_side], refs[n_side + 1:2 * n_side + 1])
    ct = u_ref.shape[0]
    ch = ct * LANES
    nq = tm // SUBLANES
    nb = taps - 1
    hb = hist_ref.shape[2]
    t = pl.program_id(1)

    @pl.when(t == 0)
    def _():
        for c in range(ct):
            for e in range(nb):
                row = hist_ref[0, c, hb - nb + e:hb - nb + e + 1, :]
                tail_ref[c, e * SUBLANES:(e + 1) * SUBLANES, :] = jnp.broadcast_to(row, (SUBLANES, LANES))

    @pl.when(t > 0)
    def _():
        tail_ref[...] = ext_ref[:, nq * SUBLANES:(nq + nb) * SUBLANES, :]

    def load(q, carry):
        dst = pl.multiple_of((nb + q) * SUBLANES, SUBLANES)
        for c in range(ct):
            ext_ref[c, pl.ds(dst, SUBLANES), :] = u_ref[c, pl.ds(q, SUBLANES, stride=nq), :]
        return carry

    jax.lax.fori_loop(0, nq, load, 0)

    first_strip = jax.lax.broadcasted_iota(jnp.int32, (SUBLANES, LANES), 0) == 0
    for c in range(ct):
        for e in range(nb):
            cur = ext_ref[c, (nq + e) * SUBLANES:(nq + e + 1) * SUBLANES, :]
            prv = tail_ref[c, e * SUBLANES:(e + 1) * SUBLANES, :]
            ext_ref[c, e * SUBLANES:(e + 1) * SUBLANES, :] = jnp.where(
                first_strip, pltpu.roll(prv, 1, 0), pltpu.roll(cur, 1, 0))

    def conv_block(ib, carry):
        rows = qb * SUBLANES
        for c in range(ct):
            lanes = slice(c * LANES, (c + 1) * LANES)
            acc = w_ref[0:1, lanes] * ext_ref[c, pl.ds(pl.multiple_of(ib * rows, rows), rows), :]
            for j in range(1, taps):
                src = pl.multiple_of(ib * rows + j * SUBLANES, SUBLANES)
                acc = acc + w_ref[j:j + 1, lanes] * ext_ref[c, pl.ds(src, rows), :]
            y_ref[c, pl.ds(pl.multiple_of(ib * rows, rows), rows), :] = acc + cb_ref[:, lanes]
        return carry

    jax.lax.fori_loop(0, nq // qb, conv_block, 0)

    for r0 in range(0, tm, BF16_ROWS):
        tiles = []
        for c in range(ct):
            halves = []
            for rr in range(r0, r0 + BF16_ROWS, SUBLANES):
                s, q = divmod(rr, nq)
                halves.append(y_ref[c, pl.ds(q * SUBLANES + s, SUBLANES, stride=SUBLANES), :])
            tiles.append(jnp.concatenate(halves, axis=0))
        for c, o in enumerate(_swish_ln_tiles(tiles, g_ref, b_ref, ch)):
            c_ref[r0:r0 + BF16_ROWS, c * LANES:(c + 1) * LANES] = o


def _conv_module(hist, u, w, cb, g, b, l, *, tm, tiles_per_seg, side=()):
    ct, rows, _ = u.shape
    ch = ct * LANES
    n_seg, _, hb, _ = hist.shape
    taps = w.shape[1]
    nq = tm // SUBLANES
    qb = 8
    assert nq % qb == 0 and hb >= taps - 1 and nq >= taps - 1
    casts = [_SideCast(sw, sl, r, n_seg * tiles_per_seg, lambda s, t: s * tiles_per_seg + t) for sw, sl, r in side]
    kern = functools.partial(_conv_kernel, tm=tm, taps=taps, qb=qb, n_side=len(casts))
    return pl.pallas_call(
        kern,
        grid=(n_seg, tiles_per_seg),
        in_specs=[pl.BlockSpec((1, ct, hb, LANES), lambda s, t: (s, 0, 0, 0)),
                  pl.BlockSpec((ct, tm, LANES), lambda s, t: (0, s * tiles_per_seg + t, 0)),
                  _layer_spec(l, taps, ch), _layer_spec(l, 1, ch), _layer_spec(l, 1, ch), _layer_spec(l, 1, ch)]
        + [c_.in_spec for c_ in casts],
        out_specs=[pl.BlockSpec((tm, ch), lambda s, t: (s * tiles_per_seg + t, 0))] + [c_.out_spec for c_ in casts],
        out_shape=[jax.ShapeDtypeStruct((rows, ch), BF16)] + [c_.out_shape for c_ in casts],
        scratch_shapes=[pltpu.VMEM((ct, tm + (taps - 1) * SUBLANES, LANES), F32),
                        pltpu.VMEM((ct, (taps - 1) * SUBLANES, LANES), F32),
                        pltpu.VMEM((ct, tm, LANES), F32)],
        compiler_params=_params("arbitrary", "arbitrary"),
        name="conv_module",
    )(hist, u, w, cb, g, b, *[c_.src for c_ in casts])


def _conv_sample_kernel(hist_ref, u_ref, w_ref, cb_ref, g_ref, b_ref, *refs):
    c_ref, ns_ref = refs[-2:]
    n_hist = hist_ref.shape[0]
    ct, t_steps = u_ref.shape[0], u_ref.shape[1]
    for k in range(ns_ref.shape[0]):
        for s in range(n_hist):
            src = s + t_steps
            if src < n_hist:
                ns_ref[k, :, s, :] = hist_ref[src]
            else:
                for c in range(ct):
                    ns_ref[k, :, s, c * LANES:(c + 1) * LANES] = u_ref[c, src - n_hist]
    for t in range(t_steps):
        tiles = []
        for c in range(ct):
            lanes = slice(c * LANES, (c + 1) * LANES)
            acc = None
            for j in range(w_ref.shape[0]):
                s = t + j
                src = hist_ref[s, :, lanes] if s < n_hist else u_ref[c, s - n_hist]
                term = w_ref[j:j + 1, lanes] * src
                acc = term if acc is None else acc + term
            tiles.append(acc + cb_ref[:, lanes])
        for c, o in enumerate(_swish_ln_tiles(tiles, g_ref, b_ref, ct * LANES)):
            c_ref[t, :, c * LANES:(c + 1) * LANES] = o


def _state_out(bufs, l, depth, block, n_inputs, first_out):
    tail = (0,) * (len(block) - 1)
    if bufs is None:
        return pl.BlockSpec((depth,) + block, lambda i: (0, i) + tail), [], [], {}
    specs = [pl.BlockSpec(memory_space=pl.ANY)] * len(bufs)
    aliases = {n_inputs + k: first_out + k for k in range(len(bufs))}
    return pl.BlockSpec((1,) + block, lambda i: (l, i) + tail), list(bufs), specs, aliases


def _conv_module_sample(hist, u, w, cb, g, b, l, state_buf, *, bb):
    depth, n_hist, n_seq, ch = hist.shape
    ct, t_steps, _, _ = u.shape
    taps = w.shape[1]
    assert n_hist == taps - 1 and ch == ct * LANES
    st_spec, bufs, buf_specs, aliases = _state_out(None if state_buf is None else [state_buf], l, depth,
                                                   (bb, n_hist, ch), 6, 1)
    return pl.pallas_call(
        _conv_sample_kernel,
        grid=(n_seq // bb,),
        in_specs=[pl.BlockSpec((None, n_hist, bb, ch), lambda i: (l, 0, i, 0)),
                  pl.BlockSpec((ct, t_steps, bb, LANES), lambda i: (0, 0, i, 0)),
                  _layer_spec(l, taps, ch), _layer_spec(l, 1, ch), _layer_spec(l, 1, ch), _layer_spec(l, 1, ch)]
        + buf_specs,
        out_specs=[pl.BlockSpec((t_steps, bb, ch), lambda i: (0, i, 0)), st_spec],
        out_shape=[jax.ShapeDtypeStruct((t_steps, n_seq, ch), BF16),
                   jax.ShapeDtypeStruct((depth, n_seq, n_hist, ch), F32)],
        input_output_aliases=aliases,
        compiler_params=_params("arbitrary"),
        name="conv_module_sample",
    )(hist, u, w, cb, g, b, *bufs)


def _attn_prompt_kernel(sink_ref, q_ref, kvp_ref, kvc_ref, *refs, n_heads, group):
    n_side = (len(refs) - 1) // 2
    o_ref = refs[n_side]
    _run_side_casts(refs[:n_side], refs[n_side + 1:])
    blk = q_ref.shape[0]
    i = pl.program_id(1)
    kv = jnp.concatenate([kvp_ref[...], kvc_ref[...]], axis=0)
    kband = kv[:, 0:LANES]
    vband = kv[:, LANES:2 * LANES]
    kroll = pltpu.roll(kband, HEAD_DIM, 1)
    vroll = pltpu.roll(vband, HEAD_DIM, 1)
    lane = jax.lax.broadcasted_iota(jnp.int32, (2 * blk, LANES), 1)
    lo = lane < HEAD_DIM

    def lo_hi(x_lo, x_hi):
        return jnp.concatenate([jnp.where(lo, x_lo, 0.0), jnp.where(lo, 0.0, x_hi)], axis=0).astype(BF16)

    kab = [lo_hi(kband, kroll), lo_hi(kroll, kband)]
    vab = [lo_hi(vband, vroll), lo_hi(vroll, vband)]

    r = jax.lax.broadcasted_iota(jnp.int32, (blk, 2 * blk), 0)
    c = jax.lax.broadcasted_iota(jnp.int32, (blk, 2 * blk), 1)
    mask = (c >= r) & (c <= r + WINDOW) & ((c >= blk) | (i > 0))
    out_lo = jax.lax.broadcasted_iota(jnp.int32, (blk, LANES), 1) < HEAD_DIM

    pairs_per_kv = group // 2
    for kvh in range(n_heads // group):
        pairs = range(kvh * pairs_per_kv, (kvh + 1) * pairs_per_kv)
        q4 = jnp.concatenate([q_ref[:, p * LANES:(p + 1) * LANES] for p in pairs], axis=0)
        s4 = jax.lax.dot_general(q4, kab[kvh], (((1,), (1,)), ((), ())), preferred_element_type=F32)
        p_rows, inv_rows = [], []
        for n, pair in enumerate(pairs):
            ps, invs = [], []
            for half in range(2):
                sink = sink_ref[2 * pair + half]
                s = jnp.where(mask, s4[n * blk:(n + 1) * blk, half * 2 * blk:(half + 1) * 2 * blk], -jnp.inf)
                m = jnp.maximum(jnp.max(s, axis=-1, keepdims=True), sink)
                p = jnp.exp(s - m)
                denom = jnp.sum(p, axis=-1, keepdims=True) + jnp.exp(sink - m)
                ps.append(p.astype(BF16))
                invs.append(1.0 / denom)
            p_rows.append(jnp.concatenate(ps, axis=1))
            inv_rows.append(jnp.where(out_lo, invs[0], invs[1]))
        o4 = jnp.dot(jnp.concatenate(p_rows, axis=0), vab[kvh], preferred_element_type=F32)
        for n, pair in enumerate(pairs):
            o_ref[:, pair * LANES:(pair + 1) * LANES] = (o4[n * blk:(n + 1) * blk] * inv_rows[n]).astype(BF16)


def _attn_prompt(sinks, q, kv, *, n_seg, blocks_per_seg, blk, side=()):
    rows, attn_w = q.shape
    casts = [_SideCast(w, wl, r, n_seg * blocks_per_seg, lambda s, i: s * blocks_per_seg + i) for w, wl, r in side]
    kv_w = kv.shape[1]
    n_heads = attn_w // HEAD_DIM
    kern = functools.partial(_attn_prompt_kernel, n_heads=n_heads, group=n_heads // N_KV_HEADS)
    return pl.pallas_call(
        kern,
        grid=(n_seg, blocks_per_seg),
        in_specs=[pl.BlockSpec(memory_space=pltpu.SMEM),
                  pl.BlockSpec((blk, attn_w), lambda s, i: (s * blocks_per_seg + i, 0)),
                  pl.BlockSpec((blk, kv_w), lambda s, i: (jnp.maximum(s * blocks_per_seg + i - 1, 0), 0)),
                  pl.BlockSpec((blk, kv_w), lambda s, i: (s * blocks_per_seg + i, 0))]
        + [c.in_spec for c in casts],
        out_specs=[pl.BlockSpec((blk, attn_w), lambda s, i: (s * blocks_per_seg + i, 0))]
        + [c.out_spec for c in casts],
        out_shape=[jax.ShapeDtypeStruct((rows, attn_w), BF16)] + [c.out_shape for c in casts],
        compiler_params=_params("arbitrary", "arbitrary"),
        name="attn_prompt",
    )(sinks, q, kv, kv, *[c.src for c in casts])


def _attn_sample_kernel(sink_ref, q_ref, hk_ref, hv_ref, nkv_ref, *refs, t_steps, group):
    o_ref, nk_ref, nv_ref = refs[-3:]
    bb, nq, _ = q_ref.shape
    npad = nkv_ref.shape[1]
    for st_ref, h_ref, lane0 in ((nk_ref, hk_ref, 0), (nv_ref, hv_ref, LANES)):
        for k in range(st_ref.shape[0]):
            st_ref[k, :, 0:WINDOW - t_steps, :] = h_ref[:, t_steps:WINDOW, :]
            st_ref[k, :, WINDOW - t_steps:WINDOW, :] = nkv_ref[:, 0:t_steps, lane0:lane0 + LANES]
    q = q_ref[...]
    hk = hk_ref[...].astype(BF16)
    hv = hv_ref[...].astype(BF16)
    nk = nkv_ref[:, :, 0:LANES].astype(BF16)
    nv = nkv_ref[:, :, LANES:2 * LANES].astype(BF16)
    s_h = jnp.einsum('bqd,bkd->bqk', q, hk, preferred_element_type=F32)
    s_n = jnp.einsum('bqd,bkd->bqk', q, nk, preferred_element_type=F32)
    row = jax.lax.broadcasted_iota(jnp.int32, (1, nq, 1), 1)
    t = (row // group) % t_steps
    kvh = row // (group * t_steps)
    head = kvh * group + row % group
    sink = jnp.zeros((1, nq, 1), F32)
    for h in range(N_KV_HEADS * group):
        sink = jnp.where(head == h, sink_ref[h], sink)
    j_h = jax.lax.broadcasted_iota(jnp.int32, (1, nq, WINDOW), 2)
    j_n = jax.lax.broadcasted_iota(jnp.int32, (1, nq, npad), 2)
    s_h = jnp.where(j_h >= t, s_h, -jnp.inf)
    s_n = jnp.where(j_n <= t, s_n, -jnp.inf)
    m = jnp.maximum(jnp.maximum(jnp.max(s_h, axis=-1, keepdims=True), jnp.max(s_n, axis=-1, keepdims=True)), sink)
    p_h = jnp.exp(s_h - m)
    p_n = jnp.exp(s_n - m)
    denom = jnp.sum(p_h, axis=-1, keepdims=True) + jnp.sum(p_n, axis=-1, keepdims=True) + jnp.exp(sink - m)
    o = (jnp.einsum('bqk,bkd->bqd', p_h.astype(BF16), hv, preferred_element_type=F32)
         + jnp.einsum('bqk,bkd->bqd', p_n.astype(BF16), nv, preferred_element_type=F32))
    lane_kvh = jax.lax.broadcasted_iota(jnp.int32, (1, nq, LANES), 2) // HEAD_DIM
    o_ref[...] = jnp.where(lane_kvh == kvh, o / denom, 0.0)


def _attn_sample(sinks, qpad, hist_k, hist_v, new_kv, l, state_bufs, *, t_steps, bb):
    n_seq, nq, _ = qpad.shape
    npad = new_kv.shape[1]
    n_heads = sinks.shape[0]
    kern = functools.partial(_attn_sample_kernel, t_steps=t_steps, group=n_heads // N_KV_HEADS)
    st_spec, bufs, buf_specs, aliases = _state_out(state_bufs, l, hist_k.shape[0], (bb, WINDOW, LANES), 5, 1)
    st_shape = jax.ShapeDtypeStruct(hist_k.shape, F32)
    return pl.pallas_call(
        kern,
        grid=(n_seq // bb,),
        in_specs=[pl.BlockSpec(memory_space=pltpu.SMEM),
                  pl.BlockSpec((bb, nq, LANES), lambda i: (i, 0, 0)),
                  pl.BlockSpec((None, bb, WINDOW, LANES), lambda i: (l, i, 0, 0)),
                  pl.BlockSpec((None, bb, WINDOW, LANES), lambda i: (l, i, 0, 0)),
                  pl.BlockSpec((bb, npad, 2 * LANES), lambda i: (i, 0, 0))] + buf_specs,
        out_specs=[pl.BlockSpec((bb, nq, LANES), lambda i: (i, 0, 0)), st_spec, st_spec],
        out_shape=[jax.ShapeDtypeStruct((n_seq, nq, LANES), F32), st_shape, st_shape],
        input_output_aliases=aliases,
        compiler_params=_params("arbitrary"),
        name="attn_sample",
    )(sinks, qpad, hist_k, hist_v, new_kv, *bufs)


def _out_proj_kernel(a_ref, c_ref, x_ref, w_ref, b_ref, g_ref, beta_ref, *refs, alpha):
    n_side = (len(refs) - 1) // 2
    o_ref = refs[n_side]
    _run_side_casts(refs[:n_side], refs[n_side + 1:])
    aw = a_ref.shape[1]
    hm = a_ref.shape[0] // 2
    for h in range(2):
        rows = slice(h * hm, (h + 1) * hm)
        mix = (jnp.dot(a_ref[rows, :], w_ref[0:aw, :], preferred_element_type=F32)
               + jnp.dot(c_ref[rows, :], w_ref[aw:, :], preferred_element_type=F32) + b_ref[...])
        o_ref[rows, :] = _layer_norm(alpha * x_ref[rows, :] + mix, g_ref[...], beta_ref[...])


def _out_proj(attn, c, x, w, b, g, beta, l, wl, *, tm, alpha, side=()):
    rows, d = x.shape
    aw, cw = attn.shape[1], c.shape[1]
    casts = [_SideCast(sw, wl, r, rows // tm, lambda i: i) for sw, wl, r in side]
    return pl.pallas_call(
        functools.partial(_out_proj_kernel, alpha=alpha),
        grid=(rows // tm,),
        in_specs=[pl.BlockSpec((tm, aw), lambda i: (i, 0)),
                  pl.BlockSpec((tm, cw), lambda i: (i, 0)),
                  pl.BlockSpec((tm, d), lambda i: (i, 0)),
                  pl.BlockSpec((None, aw + cw, d), lambda i: (wl, 0, 0), pipeline_mode=pl.Buffered(1)),
                  _layer_spec(l, 1, d), _layer_spec(l, 1, d), _layer_spec(l, 1, d)] + [c_.in_spec for c_ in casts],
        out_specs=[pl.BlockSpec((tm, d), lambda i: (i, 0))] + [c_.out_spec for c_ in casts],
        out_shape=[jax.ShapeDtypeStruct((rows, d), F32)] + [c_.out_shape for c_ in casts],
        compiler_params=_params("arbitrary"),
        name="out_proj",
    )(attn, c, x, w, b, g, beta, *[c_.src for c_ in casts])


def _ffn_kernel(x_ref, hu_ref, hg_ref, wu_ref, wg_ref, cwu_ref, cwg_ref, cbu_ref, cbg_ref, wdn_ref, g_ref, beta_ref,
                *refs, tm, prev, stride, tiles_per_seg, st0, st_rows, st_slabs, alpha, rb, n_split, dup_cols, n_side,
                halo):
    side_in, refs = refs[:n_side], refs[n_side:]
    o_ref, su_ref, sg_ref = refs[:3]
    side_out, refs = refs[3:3 + n_side], refs[3 + n_side:]
    xb_ref, hsu_ref, hsg_ref, p_ref = refs[:4]
    carry = refs[4:]
    i = pl.program_id(0)
    j = pl.program_id(1)
    nj = pl.num_programs(1)
    fc = p_ref.shape[1]
    hm = tm // n_split

    @pl.when(j == 0)
    def _():
        xb_ref[...] = x_ref[...].astype(BF16)
        o_ref[...] = jnp.zeros_like(o_ref)
        if carry:
            @pl.when(i == 0)
            def _():
                for cr in carry:
                    cr[...] = jnp.zeros_like(cr)

    for h in range(n_split):
        if halo:
            lo, hi = (0 if h == 0 else prev + h * hm), prev + (h + 1) * hm
            xs = xb_ref[lo:hi, :]
        else:
            lo, hi = prev + h * hm, prev + (h + 1) * hm
            xs = xb_ref[h * hm:(h + 1) * hm, :]
        hsu_ref[lo:hi, :] = jnp.dot(xs, wu_ref[0], preferred_element_type=F32)
        hsg_ref[lo:hi, :] = jnp.dot(xs, wg_ref[0], preferred_element_type=F32)
    _run_side_casts(side_in, side_out)

    if halo:
        pass
    elif carry:
        first = (i % tiles_per_seg) == 0
        for hs_ref, hist_ref, cr in ((hsu_ref, hu_ref, carry[0]), (hsg_ref, hg_ref, carry[1])):
            hs_ref[0:prev, :] = jnp.where(first, hist_ref[0], cr[j])
            cr[j] = hs_ref[tm:tm + prev, :]
    else:
        hsu_ref[0:prev, :] = hu_ref[0]
        hsg_ref[0:prev, :] = hg_ref[0]
    for st_ref, hs_ref in ((su_ref, hsu_ref), (sg_ref, hsg_ref)):
        v = hs_ref[prev + st0:prev + st0 + st_rows, :]
        if dup_cols:
            shifted = jnp.concatenate([v[:, dup_cols:], jnp.zeros((st_rows, dup_cols), F32)], axis=1)
            v = jnp.where(j == nj - 1, shifted, v)
        if st_slabs:
            nb = st_rows // st_slabs
            for s in range(st_slabs):
                st_ref[:, s, :] = v[s * nb:(s + 1) * nb, :]
        else:
            st_ref[0] = v

    def conv(hs_ref, cw_ref, cb_ref, r0):
        return (cw_ref[0, 2:3, :] * hs_ref[prev + r0:prev + r0 + rb, :]
                + cw_ref[0, 1:2, :] * hs_ref[prev - stride + r0:prev - stride + r0 + rb, :]
                + cw_ref[0, 0:1, :] * hs_ref[prev - 2 * stride + r0:prev - 2 * stride + r0 + rb, :]
                + cb_ref[0])

    col = jax.lax.broadcasted_iota(jnp.int32, (rb, fc), 1)
    keep = (col >= dup_cols) | (j < nj - 1)
    for h in range(n_split):
        for r0 in range(h * hm, (h + 1) * hm, rb):
            yu = conv(hsu_ref, cwu_ref, cbu_ref, r0)
            yg = conv(hsg_ref, cwg_ref, cbg_ref, r0)
            p_ref[r0:r0 + rb, :] = jnp.where(keep, yg * jax.nn.sigmoid(yg) * yu, 0.0).astype(BF16)
        rows = slice(h * hm, (h + 1) * hm)
        o_ref[rows, :] += jnp.dot(p_ref[rows, :], wdn_ref[0], preferred_element_type=F32)

    @pl.when(j == nj - 1)
    def _():
        x_rows = x_ref[prev:prev + tm, :] if halo else x_ref[...]
        o_ref[...] = _layer_norm(alpha * x_rows + o_ref[...], g_ref[...], beta_ref[...])


def _conv_ffn(x, hist, wup, cw, cb, wdn, g, beta, l, wl, *, tm, tiles_per_seg, stride, st0, st_rows, alpha, hist_base=0,
              st_slabs=0, side=(), seg_rows=None, single_buffer_rows=False, vmem_limit=VMEM_LIMIT):
    halo = seg_rows is not None
    d = x.shape[1]
    _, prev, _ = hist.shape
    d_ff = wdn.shape[1]
    fc = FF_CHUNK
    nblk = d_ff // LANES
    cblk = fc // LANES
    nj = pl.cdiv(d_ff, fc)
    dup_cols = nj * fc - d_ff
    n_tiles = x.shape[0] // seg_rows * tiles_per_seg if halo else x.shape[0] // tm
    rows = n_tiles * tm
    n_split = 2
    hm = tm // n_split
    rb = 32 if hm % 32 == 0 else 48
    assert d_ff % LANES == 0 and tm % n_split == 0 and hm % rb == 0 and nblk >= cblk
    row_mode = {"pipeline_mode": pl.Buffered(1)} if single_buffer_rows else {}

    def blk0(j):
        return jnp.minimum(j * cblk, nblk - cblk)

    el = pl.Element
    col_u = lambda j: LANES * blk0(j)
    col_g = lambda j: LANES * (nblk + blk0(j))
    casts = [_SideCast(sw, wl, r, n_tiles * nj, lambda i, j: i * nj + j) for sw, wl, r in side]
    kern = functools.partial(_ffn_kernel, tm=tm, prev=prev, stride=stride, tiles_per_seg=tiles_per_seg, st0=st0,
                             st_rows=st_rows, st_slabs=st_slabs, alpha=alpha, rb=rb, n_split=n_split,
                             dup_cols=dup_cols, n_side=len(casts), halo=halo)
    x_rows = tm + prev if halo else tm
    scratch = [pltpu.VMEM((x_rows, d), BF16), pltpu.VMEM((prev + tm, fc), F32), pltpu.VMEM((prev + tm, fc), F32),
               pltpu.VMEM((tm, fc), BF16)]
    if tiles_per_seg > 1 and not halo:
        scratch += [pltpu.VMEM((nj, prev, fc), F32), pltpu.VMEM((nj, prev, fc), F32)]
    if halo:
        assert prev % BF16_ROWS == 0 and seg_rows % BF16_ROWS == 0 and tm % BF16_ROWS == 0
        assert prev + tiles_per_seg * tm <= seg_rows
        x_spec = pl.BlockSpec((el(x_rows), el(d)), lambda i, j: (
            BF16_ROWS * ((i // tiles_per_seg) * (seg_rows // BF16_ROWS) + (i % tiles_per_seg) * (tm // BF16_ROWS)), 0),
            **row_mode)
    else:
        x_spec = pl.BlockSpec((tm, d), lambda i, j: (i, 0), **row_mode)
    hist_blk = (el(1), el(prev), el(fc))
    if st_slabs:
        assert n_tiles == 1 and st_rows % st_slabs == 0
        st_dims = (st_rows // st_slabs, st_slabs)
        st_spec = pl.BlockSpec(st_dims + (fc,), lambda i, j: (0, 0, j))
    else:
        st_dims = (n_tiles, st_rows)
        st_spec = pl.BlockSpec((1, st_rows, fc), lambda i, j: (i, 0, j))
    return pl.pallas_call(
        kern,
        grid=(n_tiles, nj),
        in_specs=[x_spec,
                  pl.BlockSpec(hist_blk, lambda i, j: (hist_base + i // tiles_per_seg, 0, col_u(j))),
                  pl.BlockSpec(hist_blk, lambda i, j: (hist_base + i // tiles_per_seg, 0, col_g(j))),
                  pl.BlockSpec((el(1), el(d), el(fc)), lambda i, j: (wl, 0, col_u(j))),
                  pl.BlockSpec((el(1), el(d), el(fc)), lambda i, j: (wl, 0, col_g(j))),
                  pl.BlockSpec((el(1), el(3), el(fc)), lambda i, j: (l, 0, col_u(j))),
                  pl.BlockSpec((el(1), el(3), el(fc)), lambda i, j: (l, 0, col_g(j))),
                  pl.BlockSpec((el(1), el(1), el(fc)), lambda i, j: (l, 0, col_u(j))),
                  pl.BlockSpec((el(1), el(1), el(fc)), lambda i, j: (l, 0, col_g(j))),
                  pl.BlockSpec((el(1), el(fc), el(d)), lambda i, j: (wl, col_u(j), 0)),
                  _layer_spec(l, 1, d), _layer_spec(l, 1, d)] + [c_.in_spec for c_ in casts],
        out_specs=[pl.BlockSpec((tm, d), lambda i, j: (i, 0), **row_mode), st_spec, st_spec]
        + [c_.out_spec for c_ in casts],
        out_shape=[jax.ShapeDtypeStruct((rows, d), F32),
                   jax.ShapeDtypeStruct(st_dims + (nj * fc,), F32),
                   jax.ShapeDtypeStruct(st_dims + (nj * fc,), F32)] + [c_.out_shape for c_ in casts],
        scratch_shapes=scratch,
        compiler_params=_params("arbitrary", "arbitrary", vmem_limit=vmem_limit),
        name="conv_ffn",
    )(x, hist, hist, wup, wup, cw, cw, cb, cb, wdn, g, beta, *[c_.src for c_ in casts])


def kernel(x_prompt, x_sample, state_attn_k, state_attn_v, state_conv, state_ffn_conv, meta_tokens, w_in, b_in, attn_sinks, conv_w, conv_b, conv_ln_g, conv_ln_b, w_out, b_out, ln1_g, ln1_b, ffn_w_up, ffn_conv_w, ffn_conv_b, ffn_w_down, ln2_g, ln2_b):
    batch, seq, d_model = x_prompt.shape
    dec_batch, dec_seq, _ = x_sample.shape
    depth = w_in.shape[0]
    conv_ch = conv_w.shape[2]
    conv_taps = conv_w.shape[1]
    ffn_taps = ffn_conv_w.shape[1]
    attn_w = d_model - conv_ch
    kv_w = 2 * N_KV_HEADS * HEAD_DIM
    n_heads = attn_w // HEAD_DIM
    group = n_heads // N_KV_HEADS
    d_ff = ffn_w_down.shape[1]
    ct = conv_ch // LANES
    alpha = (2 * depth) ** 0.25
    assert ffn_taps == 3 and kv_w == 2 * LANES and dec_batch % SUBLANES == 0
    assert ffn_taps - 1 <= dec_seq <= min(conv_taps - 1, SUBLANES) and seq >= WINDOW >= conv_taps

    seq_all = N_META + seq
    lp = _round_up(seq_all, WINDOW)
    tiles_p = 6
    tm_p = lp // tiles_p
    assert tm_p % 64 == 0
    meta = jnp.broadcast_to(meta_tokens[None].astype(x_prompt.dtype), (batch, N_META, d_model))
    xp = jnp.concatenate([meta, x_prompt, jnp.zeros((batch, lp - seq_all, d_model), x_prompt.dtype)], axis=1)
    xp = xp.reshape(batch * lp, d_model)
    rows_s = dec_seq * dec_batch
    xs = jnp.swapaxes(x_sample, 0, 1).reshape(rows_s, d_model)

    conv_hist_rows = 32
    ffn_prev_p = SUBLANES
    tiles_f = 4
    tm_f = lp // tiles_f
    assert tm_f % (2 * BF16_ROWS) == 0
    last_p = seq_all - 1
    ffn_state_tile = last_p // tm_f
    ffn_st0 = (last_p % tm_f) // SUBLANES * SUBLANES
    assert (last_p - 1) // tm_f == ffn_state_tile and (last_p - 1) % tm_f >= ffn_st0

    rows3 = lambda v: v[:, None, :]
    b_in3, conv_b3, cg3, cbt3 = rows3(b_in), rows3(conv_b), rows3(conv_ln_g), rows3(conv_ln_b)
    b_out3, g1, be1, g2, be2 = rows3(b_out), rows3(ln1_g), rows3(ln1_b), rows3(ln2_g), rows3(ln2_b)
    fcb3 = rows3(ffn_conv_b)
    zero_conv_hist = jnp.zeros((batch, ct, conv_hist_rows, LANES), F32)
    zero_ffn_hist = jnp.zeros((batch, ffn_prev_p, 2 * d_ff), F32)
    tm_y = 1024
    tiles_y = seq // tm_y
    assert seq % tm_y == 0 and N_META % BF16_ROWS == 0
    zero_ffn_halo = jnp.zeros((batch, N_META, 2 * d_ff), F32)
    hk_all = state_attn_k.reshape(depth, dec_batch, WINDOW, LANES)
    hv_all = state_attn_v.reshape(depth, dec_batch, WINDOW, LANES)
    conv_hist_all = jnp.swapaxes(state_conv, 1, 2)
    st_rows = (ffn_taps - 1) * dec_batch
    ffn_hist_all = jnp.swapaxes(state_ffn_conv, 1, 2).reshape(depth, st_rows, 2 * d_ff)

    w_in_b = w_in[:1].astype(BF16)
    w_out_b = w_out[:1].astype(BF16)
    w_up_b = w_dn_b = None

    pk, pv, pc, pf, sf = [], [], [], [], []
    kv_state = conv_state = None
    for l in range(depth):
        sinks = attn_sinks[l]
        first = l == 0
        nxt = l + 1 < depth

        q, kv, u = _in_proj(xp, w_in_b, b_in3, l, 0, tm=tm_p, attn_w=attn_w, conv_ch=conv_ch, kv_w=kv_w)
        attn, = _attn_prompt(sinks, q, kv, n_seg=batch, blocks_per_seg=lp // WINDOW, blk=WINDOW)
        c, *cast = _conv_module(zero_conv_hist, u, conv_w, conv_b3, cg3, cbt3, l, tm=tm_p, tiles_per_seg=tiles_p,
                                side=[(ffn_w_up, 0, 176)] if first else [])
        if first:
            w_up_b = cast[0][None]
        x1, *cast = _out_proj(attn, c, xp, w_out_b, b_out3, g1, be1, l, 0, tm=tm_p, alpha=alpha,
                              side=[(ffn_w_down, 0, 512)] if first else [])
        if first:
            w_dn_b = cast[0][None]
        if nxt:
            next_w = [(w_in, l + 1, 32), (w_out, l + 1, 32), (ffn_w_up, l + 1, 32), (ffn_w_down, l + 1, 64)]
            xp, su, sg, *cast = _conv_ffn(x1, zero_ffn_hist, w_up_b, ffn_conv_w, fcb3, w_dn_b, g2, be2, l, 0, tm=tm_f,
                                          tiles_per_seg=tiles_f, stride=1, st0=ffn_st0, st_rows=SUBLANES,
                                          alpha=alpha, side=next_w, single_buffer_rows=True,
                                          vmem_limit=FFN_VMEM_LIMIT)
            st_tile, st_off, st_tiles = ffn_state_tile, last_p % tm_f - ffn_st0 - (ffn_taps - 2), tiles_f
        else:
            y_rows, su, sg = _conv_ffn(x1, zero_ffn_halo, w_up_b, ffn_conv_w, fcb3, w_dn_b, g2, be2, l, 0, tm=tm_y,
                                       tiles_per_seg=tiles_y, stride=1, st0=tm_y - SUBLANES, st_rows=SUBLANES,
                                       alpha=alpha, seg_rows=lp, single_buffer_rows=True)
            st_tile, st_off, st_tiles = tiles_y - 1, SUBLANES - (ffn_taps - 1), tiles_y
        kv3 = kv.reshape(batch, lp, kv_w)[:, seq_all - WINDOW:seq_all]
        pk.append(kv3[..., :kv_w // 2].reshape(batch, WINDOW, N_KV_HEADS, HEAD_DIM))
        pv.append(kv3[..., kv_w // 2:].reshape(batch, WINDOW, N_KV_HEADS, HEAD_DIM))
        u_tail = u.reshape(ct, batch, lp, LANES)[:, :, seq_all - (conv_taps - 1):seq_all]
        pc.append(u_tail.transpose(1, 2, 0, 3).reshape(batch, conv_taps - 1, conv_ch))
        hst = jnp.concatenate([su[..., :d_ff], sg[..., :d_ff]], axis=-1).reshape(batch, st_tiles, SUBLANES, 2 * d_ff)
        pf.append(hst[:, st_tile, st_off:st_off + ffn_taps - 1])

        q, kv, u = _in_proj(xs, w_in_b, b_in3, l, 0, tm=rows_s, attn_w=attn_w, conv_ch=conv_ch, kv_w=kv_w)
        q5 = q.reshape(dec_seq, dec_batch, N_KV_HEADS, group, HEAD_DIM).transpose(1, 2, 0, 3, 4)
        q5 = q5.reshape(dec_batch, N_KV_HEADS, dec_seq * group, HEAD_DIM)
        zq = jnp.zeros_like(q5[:, 0])
        qpad = jnp.concatenate([jnp.concatenate([q5[:, 0], zq], axis=-1),
                                jnp.concatenate([zq, q5[:, 1]], axis=-1)], axis=1)
        kv_new = kv.reshape(dec_seq, dec_batch, kv_w).transpose(1, 0, 2)
        kv_new_pad = jnp.pad(kv_new, ((0, 0), (0, SUBLANES - dec_seq), (0, 0)))
        o, *kv_state = _attn_sample(sinks, qpad, hk_all, hv_all, kv_new_pad, l, kv_state, t_steps=dec_seq, bb=8)
        o = (o[..., :HEAD_DIM] + o[..., HEAD_DIM:]).reshape(dec_batch, N_KV_HEADS, dec_seq, group, HEAD_DIM)
        attn = o.transpose(2, 0, 1, 3, 4).reshape(rows_s, attn_w).astype(BF16)
        u4 = u.reshape(ct, dec_seq, dec_batch, LANES)
        c, conv_state = _conv_module_sample(conv_hist_all, u4, conv_w, conv_b3, cg3, cbt3, l, conv_state, bb=32)
        c = c.reshape(rows_s, conv_ch)
        x1, = _out_proj(attn, c, xs, w_out_b, b_out3, g1, be1, l, 0, tm=rows_s, alpha=alpha)
        xs, su, sg = _conv_ffn(x1, ffn_hist_all, w_up_b, ffn_conv_w, fcb3, w_dn_b, g2, be2, l, 0, tm=rows_s,
                               tiles_per_seg=1, stride=dec_batch, st0=rows_s - st_rows, st_rows=st_rows,
                               alpha=alpha, hist_base=l, st_slabs=ffn_taps - 1)
        sf.append(jnp.concatenate([su[..., :d_ff], sg[..., :d_ff]], axis=-1))
        if nxt:
            w_in_b, w_out_b, w_up_b, w_dn_b = (w[None] for w in cast)

    y_prompt = y_rows.reshape(batch, seq, d_model)
    y_sample = jnp.swapaxes(xs.reshape(dec_seq, dec_batch, d_model), 0, 1)
    kv_shape = (depth, dec_batch, WINDOW, N_KV_HEADS, HEAD_DIM)
    return (y_prompt, y_sample, jnp.stack(pk), jnp.stack(pv), jnp.stack(pc), jnp.stack(pf),
            kv_state[0].reshape(kv_shape), kv_state[1].reshape(kv_shape), conv_state, jnp.stack(sf))
```

```python
import functools

import jax
import jax.numpy as jnp
from jax.experimental import pallas as pl
from jax.experimental.pallas import tpu as pltpu

F32 = jnp.float32
BF16 = jnp.bfloat16

N_META = 16
HEAD_DIM = 64
N_KV_HEADS = 2
WINDOW = 128
LN_EPS = 1e-5
LANES = 128
SUBLANES = 8
BF16_ROWS = 16
FF_CHUNK = 512
VMEM_LIMIT = 56 * 1024 * 1024


def _round_up(x, m):
    return (x + m - 1) // m * m


def _layer_norm(y, g, b):
    mu = jnp.mean(y, axis=-1, keepdims=True)
    d = y - mu
    var = jnp.mean(d * d, axis=-1, keepdims=True)
    return d * jax.lax.rsqrt(var + LN_EPS) * g + b


def _params(*sem):
    return pltpu.CompilerParams(dimension_semantics=sem, vmem_limit_bytes=VMEM_LIMIT)


def _layer_spec(l, *block):
    zeros = (0,) * len(block)
    return pl.BlockSpec((None,) + block, lambda *_: (l,) + zeros)


class _SideCast:
    def __init__(self, src, layer, chunk_rows, n_steps, step_of):
        _, rows, cols = src.shape
        n_chunks = pl.cdiv(rows, chunk_rows)
        assert n_chunks <= n_steps and chunk_rows % BF16_ROWS == 0
        self.src = src
        chunk = lambda *g: jnp.minimum(step_of(*g), n_chunks - 1)
        self.in_spec = pl.BlockSpec((None, chunk_rows, cols), lambda *g: (layer, chunk(*g), 0))
        self.out_spec = pl.BlockSpec((chunk_rows, cols), lambda *g: (chunk(*g), 0))
        self.out_shape = jax.ShapeDtypeStruct((rows, cols), BF16)


def _run_side_casts(side_in, side_out):
    for src_ref, dst_ref in zip(side_in, side_out):
        dst_ref[...] = src_ref[...].astype(BF16)


def _in_proj_kernel(x_ref, w_ref, b_ref, q_ref, kv_ref, u_ref, *, attn_w, conv_ch, kv_w, q_scale):
    xb = x_ref[...].astype(BF16)

    def proj(c0, n):
        return jnp.dot(xb, w_ref[:, c0:c0 + n], preferred_element_type=F32) + b_ref[:, c0:c0 + n]

    q_ref[...] = (proj(0, attn_w) * q_scale).astype(BF16)
    kv_ref[...] = proj(attn_w, kv_w)
    a = proj(attn_w + kv_w, conv_ch)
    g = proj(attn_w + kv_w + conv_ch, conv_ch)
    u = a * jax.nn.sigmoid(g)
    for c in range(conv_ch // LANES):
        u_ref[c] = u[:, c * LANES:(c + 1) * LANES]


def _in_proj(x, w, b, l, wl, *, tm, attn_w, conv_ch, kv_w):
    rows, d = x.shape
    n = w.shape[2]
    ct = conv_ch // LANES
    kern = functools.partial(_in_proj_kernel, attn_w=attn_w, conv_ch=conv_ch, kv_w=kv_w,
                             q_scale=HEAD_DIM ** -0.5)
    return pl.pallas_call(
        kern,
        grid=(rows // tm,),
        in_specs=[pl.BlockSpec((tm, d), lambda i: (i, 0)),
                  pl.BlockSpec((None, d, n), lambda i: (wl, 0, 0), pipeline_mode=pl.Buffered(1)),
                  _layer_spec(l, 1, n)],
        out_specs=[pl.BlockSpec((tm, attn_w), lambda i: (i, 0)),
                   pl.BlockSpec((tm, kv_w), lambda i: (i, 0)),
                   pl.BlockSpec((ct, tm, LANES), lambda i: (0, i, 0))],
        out_shape=[jax.ShapeDtypeStruct((rows, attn_w), BF16),
                   jax.ShapeDtypeStruct((rows, kv_w), F32),
                   jax.ShapeDtypeStruct((ct, rows, LANES), F32)],
        compiler_params=_params("arbitrary"),
        name="in_proj",
    )(x, w, b)


def _swish_ln_tiles(tiles, g_ref, b_ref, ch):
    total = tiles[0]
    for tl in tiles[1:]:
        total = total + tl
    mu = jnp.sum(total, axis=-1, keepdims=True) / ch
    ds = [tl - mu for tl in tiles]
    sq = ds[0] * ds[0]
    for d in ds[1:]:
        sq = sq + d * d
    inv = jax.lax.rsqrt(jnp.sum(sq, axis=-1, keepdims=True) / ch + LN_EPS)
    out = []
    for c, d in enumerate(ds):
        y = d * inv * g_ref[:, c * LANES:(c + 1) * LANES] + b_ref[:, c * LANES:(c + 1) * LANES]
        out.append((y * jax.nn.sigmoid(y)).astype(BF16))
    return out


def _conv_kernel(hist_ref, u_ref, w_ref, cb_ref, g_ref, b_ref, *refs, tm, taps, qb, n_side):
    c_ref = refs[n_side]
    ext_ref, tail_ref, y_ref = refs[2 * n_side + 1:]
    _run_side_casts(refs[:n_side], refs[n_side + 1:2 * n_side + 1])
    ct = u_ref.shape[0]
    ch = ct * LANES
    nq = tm // SUBLANES
    nb = taps - 1
    hb = hist_ref.shape[2]
    t = pl.program_id(1)

    @pl.when(t == 0)
    def _():
        for c in range(ct):
            for e in range(nb):
                row = hist_ref[0, c, hb - nb + e:hb - nb + e + 1, :]
                tail_ref[c, e * SUBLANES:(e + 1) * SUBLANES, :] = jnp.broadcast_to(row, (SUBLANES, LANES))

    @pl.when(t > 0)
    def _():
        tail_ref[...] = ext_ref[:, nq * SUBLANES:(nq + nb) * SUBLANES, :]

    def load(q, carry):
        dst = pl.multiple_of((nb + q) * SUBLANES, SUBLANES)
        for c in range(ct):
            ext_ref[c, pl.ds(dst, SUBLANES), :] = u_ref[c, pl.ds(q, SUBLANES, stride=nq), :]
        return carry

    jax.lax.fori_loop(0, nq, load, 0)

    first_strip = jax.lax.broadcasted_iota(jnp.int32, (SUBLANES, LANES), 0) == 0
    for c in range(ct):
        for e in range(nb):
            cur = ext_ref[c, (nq + e) * SUBLANES:(nq + e + 1) * SUBLANES, :]
            prv = tail_ref[c, e * SUBLANES:(e + 1) * SUBLANES, :]
            ext_ref[c, e * SUBLANES:(e + 1) * SUBLANES, :] = jnp.where(
                first_strip, pltpu.roll(prv, 1, 0), pltpu.roll(cur, 1, 0))

    def conv_block(ib, carry):
        rows = qb * SUBLANES
        for c in range(ct):
            lanes = slice(c * LANES, (c + 1) * LANES)
            acc = w_ref[0:1, lanes] * ext_ref[c, pl.ds(pl.multiple_of(ib * rows, rows), rows), :]
            for j in range(1, taps):
                src = pl.multiple_of(ib * rows + j * SUBLANES, SUBLANES)
                acc = acc + w_ref[j:j + 1, lanes] * ext_ref[c, pl.ds(src, rows), :]
            y_ref[c, pl.ds(pl.multiple_of(ib * rows, rows), rows), :] = acc + cb_ref[:, lanes]
        return carry

    jax.lax.fori_loop(0, nq // qb, conv_block, 0)

    for r0 in range(0, tm, BF16_ROWS):
        tiles = []
        for c in range(ct):
            halves = []
            for rr in range(r0, r0 + BF16_ROWS, SUBLANES):
                s, q = divmod(rr, nq)
                halves.append(y_ref[c, pl.ds(q * SUBLANES + s, SUBLANES, stride=SUBLANES), :])
            tiles.append(jnp.concatenate(halves, axis=0))
        for c, o in enumerate(_swish_ln_tiles(tiles, g_ref, b_ref, ch)):
            c_ref[r0:r0 + BF16_ROWS, c * LANES:(c + 1) * LANES] = o


def _conv_module(hist, u, w, cb, g, b, l, *, tm, tiles_per_seg, side=()):
    ct, rows, _ = u.shape
    ch = ct * LANES
    n_seg, _, hb, _ = hist.shape
    taps = w.shape[1]
    nq = tm // SUBLANES
    qb = 8
    assert nq % qb == 0 and hb >= taps - 1 and nq >= taps - 1
    casts = [_SideCast(sw, sl, r, n_seg * tiles_per_seg, lambda s, t: s * tiles_per_seg + t) for sw, sl, r in side]
    kern = functools.partial(_conv_kernel, tm=tm, taps=taps, qb=qb, n_side=len(casts))
    return pl.pallas_call(
        kern,
        grid=(n_seg, tiles_per_seg),
        in_specs=[pl.BlockSpec((1, ct, hb, LANES), lambda s, t: (s, 0, 0, 0)),
                  pl.BlockSpec((ct, tm, LANES), lambda s, t: (0, s * tiles_per_seg + t, 0)),
                  _layer_spec(l, taps, ch), _layer_spec(l, 1, ch), _layer_spec(l, 1, ch), _layer_spec(l, 1, ch)]
        + [c_.in_spec for c_ in casts],
        out_specs=[pl.BlockSpec((tm, ch), lambda s, t: (s * tiles_per_seg + t, 0))] + [c_.out_spec for c_ in casts],
        out_shape=[jax.ShapeDtypeStruct((rows, ch), BF16)] + [c_.out_shape for c_ in casts],
        scratch_shapes=[pltpu.VMEM((ct, tm + (taps - 1) * SUBLANES, LANES), F32),
                        pltpu.VMEM((ct, (taps - 1) * SUBLANES, LANES), F32),
                        pltpu.VMEM((ct, tm, LANES), F32)],
        compiler_params=_params("arbitrary", "arbitrary"),
        name="conv_module",
    )(hist, u, w, cb, g, b, *[c_.src for c_ in casts])


def _conv_sample_kernel(hist_ref, u_ref, w_ref, cb_ref, g_ref, b_ref, *refs):
    c_ref, ns_ref = refs[-2:]
    n_hist = hist_ref.shape[0]
    ct, t_steps = u_ref.shape[0], u_ref.shape[1]
    for k in range(ns_ref.shape[0]):
        for s in range(n_hist):
            src = s + t_steps
            if src < n_hist:
                ns_ref[k, :, s, :] = hist_ref[src]
            else:
                for c in range(ct):
                    ns_ref[k, :, s, c * LANES:(c + 1) * LANES] = u_ref[c, src - n_hist]
    for t in range(t_steps):
        tiles = []
        for c in range(ct):
            lanes = slice(c * LANES, (c + 1) * LANES)
            acc = None
            for j in range(w_ref.shape[0]):
                s = t + j
                src = hist_ref[s, :, lanes] if s < n_hist else u_ref[c, s - n_hist]
                term = w_ref[j:j + 1, lanes] * src
                acc = term if acc is None else acc + term
            tiles.append(acc + cb_ref[:, lanes])
        for c, o in enumerate(_swish_ln_tiles(tiles, g_ref, b_ref, ct * LANES)):
            c_ref[t, :, c * LANES:(c + 1) * LANES] = o


def _state_out(bufs, l, depth, block, n_inputs, first_out):
    tail = (0,) * (len(block) - 1)
    if bufs is None:
        return pl.BlockSpec((depth,) + block, lambda i: (0, i) + tail), [], [], {}
    specs = [pl.BlockSpec(memory_space=pl.ANY)] * len(bufs)
    aliases = {n_inputs + k: first_out + k for k in range(len(bufs))}
    return pl.BlockSpec((1,) + block, lambda i: (l, i) + tail), list(bufs), specs, aliases


def _conv_module_sample(hist, u, w, cb, g, b, l, state_buf, *, bb):
    depth, n_hist, n_seq, ch = hist.shape
    ct, t_steps, _, _ = u.shape
    taps = w.shape[1]
    assert n_hist == taps - 1 and ch == ct * LANES
    st_spec, bufs, buf_specs, aliases = _state_out(None if state_buf is None else [state_buf], l, depth,
                                                   (bb, n_hist, ch), 6, 1)
    return pl.pallas_call(
        _conv_sample_kernel,
        grid=(n_seq // bb,),
        in_specs=[pl.BlockSpec((None, n_hist, bb, ch), lambda i: (l, 0, i, 0)),
                  pl.BlockSpec((ct, t_steps, bb, LANES), lambda i: (0, 0, i, 0)),
                  _layer_spec(l, taps, ch), _layer_spec(l, 1, ch), _layer_spec(l, 1, ch), _layer_spec(l, 1, ch)]
        + buf_specs,
        out_specs=[pl.BlockSpec((t_steps, bb, ch), lambda i: (0, i, 0)), st_spec],
        out_shape=[jax.ShapeDtypeStruct((t_steps, n_seq, ch), BF16),
                   jax.ShapeDtypeStruct((depth, n_seq, n_hist, ch), F32)],
        input_output_aliases=aliases,
        compiler_params=_params("arbitrary"),
        name="conv_module_sample",
    )(hist, u, w, cb, g, b, *bufs)


def _attn_prompt_kernel(sink_ref, q_ref, kvp_ref, kvc_ref, *refs, n_heads, group):
    n_side = (len(refs) - 1) // 2
    o_ref = refs[n_side]
    _run_side_casts(refs[:n_side], refs[n_side + 1:])
    blk = q_ref.shape[0]
    i = pl.program_id(1)
    kv = jnp.concatenate([kvp_ref[...], kvc_ref[...]], axis=0)
    kband = kv[:, 0:LANES]
    vband = kv[:, LANES:2 * LANES]
    kroll = pltpu.roll(kband, HEAD_DIM, 1)
    vroll = pltpu.roll(vband, HEAD_DIM, 1)
    lane = jax.lax.broadcasted_iota(jnp.int32, (2 * blk, LANES), 1)
    lo = lane < HEAD_DIM

    def lo_hi(x_lo, x_hi):
        return jnp.concatenate([jnp.where(lo, x_lo, 0.0), jnp.where(lo, 0.0, x_hi)], axis=0).astype(BF16)

    kab = [lo_hi(kband, kroll), lo_hi(kroll, kband)]
    vab = [lo_hi(vband, vroll), lo_hi(vroll, vband)]

    r = jax.lax.broadcasted_iota(jnp.int32, (blk, 2 * blk), 0)
    c = jax.lax.broadcasted_iota(jnp.int32, (blk, 2 * blk), 1)
    mask = (c >= r) & (c <= r + WINDOW) & ((c >= blk) | (i > 0))
    out_lo = jax.lax.broadcasted_iota(jnp.int32, (blk, LANES), 1) < HEAD_DIM

    pairs_per_kv = group // 2
    for kvh in range(n_heads // group):
        pairs = range(kvh * pairs_per_kv, (kvh + 1) * pairs_per_kv)
        q4 = jnp.concatenate([q_ref[:, p * LANES:(p + 1) * LANES] for p in pairs], axis=0)
        s4 = jax.lax.dot_general(q4, kab[kvh], (((1,), (1,)), ((), ())), preferred_element_type=F32)
        p_rows, inv_rows = [], []
        for n, pair in enumerate(pairs):
            ps, invs = [], []
            for half in range(2):
                sink = sink_ref[2 * pair + half]
                s = jnp.where(mask, s4[n * blk:(n + 1) * blk, half * 2 * blk:(half + 1) * 2 * blk], -jnp.inf)
                m = jnp.maximum(jnp.max(s, axis=-1, keepdims=True), sink)
                p = jnp.exp(s - m)
                denom = jnp.sum(p, axis=-1, keepdims=True) + jnp.exp(sink - m)
                ps.append(p.astype(BF16))
                invs.append(1.0 / denom)
            p_rows.append(jnp.concatenate(ps, axis=1))
            inv_rows.append(jnp.where(out_lo, invs[0], invs[1]))
        o4 = jnp.dot(jnp.concatenate(p_rows, axis=0), vab[kvh], preferred_element_type=F32)
        for n, pair in enumerate(pairs):
            o_ref[:, pair * LANES:(pair + 1) * LANES] = (o4[n * blk:(n + 1) * blk] * inv_rows[n]).astype(BF16)


def _attn_prompt(sinks, q, kv, *, n_seg, blocks_per_seg, blk, side=()):
    rows, attn_w = q.shape
    casts = [_SideCast(w, wl, r, n_seg * blocks_per_seg, lambda s, i: s * blocks_per_seg + i) for w, wl, r in side]
    kv_w = kv.shape[1]
    n_heads = attn_w // HEAD_DIM
    kern = functools.partial(_attn_prompt_kernel, n_heads=n_heads, group=n_heads // N_KV_HEADS)
    return pl.pallas_call(
        kern,
        grid=(n_seg, blocks_per_seg),
        in_specs=[pl.BlockSpec(memory_space=pltpu.SMEM),
                  pl.BlockSpec((blk, attn_w), lambda s, i: (s * blocks_per_seg + i, 0)),
                  pl.BlockSpec((blk, kv_w), lambda s, i: (jnp.maximum(s * blocks_per_seg + i - 1, 0), 0)),
                  pl.BlockSpec((blk, kv_w), lambda s, i: (s * blocks_per_seg + i, 0))]
        + [c.in_spec for c in casts],
        out_specs=[pl.BlockSpec((blk, attn_w), lambda s, i: (s * blocks_per_seg + i, 0))]
        + [c.out_spec for c in casts],
        out_shape=[jax.ShapeDtypeStruct((rows, attn_w), BF16)] + [c.out_shape for c in casts],
        compiler_params=_params("arbitrary", "arbitrary"),
        name="attn_prompt",
    )(sinks, q, kv, kv, *[c.src for c in casts])


def _attn_sample_kernel(sink_ref, q_ref, hk_ref, hv_ref, nk_ref, nv_ref, *refs, t_steps, group):
    o_ref, sk_ref, sv_ref = refs[-3:]
    nq = q_ref.shape[2]
    npad = nk_ref.shape[1]
    hk_all, hv_all, nk_all, nv_all = hk_ref[...], hv_ref[...], nk_ref[...], nv_ref[...]
    for st_ref, h, n in ((sk_ref, hk_all, nk_all), (sv_ref, hv_all, nv_all)):
        for k in range(st_ref.shape[0]):
            st_ref[k, :, 0:WINDOW - t_steps] = h[:, t_steps:WINDOW]
            st_ref[k, :, WINDOW - t_steps:WINDOW] = n[:, 0:t_steps]
    row = jax.lax.broadcasted_iota(jnp.int32, (1, nq, 1), 1)
    t = row // group
    j_h = jax.lax.broadcasted_iota(jnp.int32, (1, nq, WINDOW), 2)
    j_n = jax.lax.broadcasted_iota(jnp.int32, (1, nq, npad), 2)
    for g in range(N_KV_HEADS):
        q = q_ref[:, g]
        hk, hv = hk_all[:, :, g, :].astype(BF16), hv_all[:, :, g, :].astype(BF16)
        nk, nv = nk_all[:, :, g, :].astype(BF16), nv_all[:, :, g, :].astype(BF16)
        sink = jnp.zeros((1, nq, 1), F32)
        for hh in range(group):
            sink = jnp.where(row % group == hh, sink_ref[g * group + hh], sink)
        s_h = jnp.einsum('bqd,bkd->bqk', q, hk, preferred_element_type=F32)
        s_n = jnp.einsum('bqd,bkd->bqk', q, nk, preferred_element_type=F32)
        s_h = jnp.where(j_h >= t, s_h, -jnp.inf)
        s_n = jnp.where(j_n <= t, s_n, -jnp.inf)
        m = jnp.maximum(jnp.maximum(jnp.max(s_h, axis=-1, keepdims=True), jnp.max(s_n, axis=-1, keepdims=True)), sink)
        p_h = jnp.exp(s_h - m)
        p_n = jnp.exp(s_n - m)
        denom = jnp.sum(p_h, axis=-1, keepdims=True) + jnp.sum(p_n, axis=-1, keepdims=True) + jnp.exp(sink - m)
        o = (jnp.einsum('bqk,bkd->bqd', p_h.astype(BF16), hv, preferred_element_type=F32)
             + jnp.einsum('bqk,bkd->bqd', p_n.astype(BF16), nv, preferred_element_type=F32))
        o_ref[:, g] = o / denom


def _attn_sample(sinks, q, hist_k, hist_v, new_k, new_v, l, state_bufs, *, t_steps, bb):
    n_seq, n_kv, nq, hd = q.shape
    npad = new_k.shape[1]
    n_heads = sinks.shape[0]
    kern = functools.partial(_attn_sample_kernel, t_steps=t_steps, group=n_heads // n_kv)
    st_spec, bufs, buf_specs, aliases = _state_out(state_bufs, l, hist_k.shape[0], (bb, WINDOW, n_kv, hd), 6, 1)
    st_shape = jax.ShapeDtypeStruct(hist_k.shape, F32)
    hist_spec = pl.BlockSpec((None, bb, WINDOW, n_kv, hd), lambda i: (l, i, 0, 0, 0))
    new_spec = pl.BlockSpec((bb, npad, n_kv, hd), lambda i: (i, 0, 0, 0))
    return pl.pallas_call(
        kern,
        grid=(n_seq // bb,),
        in_specs=[pl.BlockSpec(memory_space=pltpu.SMEM),
                  pl.BlockSpec((bb, n_kv, nq, hd), lambda i: (i, 0, 0, 0)),
                  hist_spec, hist_spec, new_spec, new_spec] + buf_specs,
        out_specs=[pl.BlockSpec((bb, n_kv, nq, hd), lambda i: (i, 0, 0, 0)), st_spec, st_spec],
        out_shape=[jax.ShapeDtypeStruct((n_seq, n_kv, nq, hd), F32), st_shape, st_shape],
        input_output_aliases=aliases,
        compiler_params=_params("arbitrary"),
        name="attn_sample",
    )(sinks, q, hist_k, hist_v, new_k, new_v, *bufs)


def _out_proj_kernel(a_ref, c_ref, x_ref, w_ref, b_ref, g_ref, beta_ref, *refs, alpha):
    n_side = (len(refs) - 1) // 2
    o_ref = refs[n_side]
    _run_side_casts(refs[:n_side], refs[n_side + 1:])
    aw = a_ref.shape[1]
    hm = a_ref.shape[0] // 2
    for h in range(2):
        rows = slice(h * hm, (h + 1) * hm)
        mix = (jnp.dot(a_ref[rows, :], w_ref[0:aw, :], preferred_element_type=F32)
               + jnp.dot(c_ref[rows, :], w_ref[aw:, :], preferred_element_type=F32) + b_ref[...])
        o_ref[rows, :] = _layer_norm(alpha * x_ref[rows, :] + mix, g_ref[...], beta_ref[...])


def _out_proj(attn, c, x, w, b, g, beta, l, wl, *, tm, alpha, side=()):
    rows, d = x.shape
    aw, cw = attn.shape[1], c.shape[1]
    casts = [_SideCast(sw, wl, r, rows // tm, lambda i: i) for sw, wl, r in side]
    return pl.pallas_call(
        functools.partial(_out_proj_kernel, alpha=alpha),
        grid=(rows // tm,),
        in_specs=[pl.BlockSpec((tm, aw), lambda i: (i, 0)),
                  pl.BlockSpec((tm, cw), lambda i: (i, 0)),
                  pl.BlockSpec((tm, d), lambda i: (i, 0)),
                  pl.BlockSpec((None, aw + cw, d), lambda i: (wl, 0, 0), pipeline_mode=pl.Buffered(1)),
                  _layer_spec(l, 1, d), _layer_spec(l, 1, d), _layer_spec(l, 1, d)] + [c_.in_spec for c_ in casts],
        out_specs=[pl.BlockSpec((tm, d), lambda i: (i, 0))] + [c_.out_spec for c_ in casts],
        out_shape=[jax.ShapeDtypeStruct((rows, d), F32)] + [c_.out_shape for c_ in casts],
        compiler_params=_params("arbitrary"),
        name="out_proj",
    )(attn, c, x, w, b, g, beta, *[c_.src for c_ in casts])


def _ffn_kernel(x_ref, hu_ref, hg_ref, wu_ref, wg_ref, cwu_ref, cwg_ref, cbu_ref, cbg_ref, wdn_ref, g_ref, beta_ref,
                *refs, tm, prev, stride, tiles_per_seg, st0, st_rows, st_slabs, alpha, rb, n_split, dup_cols, n_side,
                halo):
    side_in, refs = refs[:n_side], refs[n_side:]
    o_ref, su_ref, sg_ref = refs[:3]
    side_out, refs = refs[3:3 + n_side], refs[3 + n_side:]
    xb_ref, hsu_ref, hsg_ref, p_ref = refs[:4]
    carry = refs[4:]
    i = pl.program_id(0)
    j = pl.program_id(1)
    nj = pl.num_programs(1)
    fc = p_ref.shape[1]
    hm = tm // n_split

    @pl.when(j == 0)
    def _():
        xb_ref[...] = x_ref[...].astype(BF16)
        o_ref[...] = jnp.zeros_like(o_ref)
        if carry:
            @pl.when(i == 0)
            def _():
                for cr in carry:
                    cr[...] = jnp.zeros_like(cr)

    for h in range(n_split):
        if halo:
            lo, hi = (0 if h == 0 else prev + h * hm), prev + (h + 1) * hm
            xs = xb_ref[lo:hi, :]
        else:
            lo, hi = prev + h * hm, prev + (h + 1) * hm
            xs = xb_ref[h * hm:(h + 1) * hm, :]
        hsu_ref[lo:hi, :] = jnp.dot(xs, wu_ref[0], preferred_element_type=F32)
        hsg_ref[lo:hi, :] = jnp.dot(xs, wg_ref[0], preferred_element_type=F32)
    _run_side_casts(side_in, side_out)

    if halo:
        pass
    elif carry:
        first = (i % tiles_per_seg) == 0
        for hs_ref, hist_ref, cr in ((hsu_ref, hu_ref, carry[0]), (hsg_ref, hg_ref, carry[1])):
            hs_ref[0:prev, :] = jnp.where(first, hist_ref[0], cr[j])
            cr[j] = hs_ref[tm:tm + prev, :]
    else:
        hsu_ref[0:prev, :] = hu_ref[0]
        hsg_ref[0:prev, :] = hg_ref[0]
    for st_ref, hs_ref in ((su_ref, hsu_ref), (sg_ref, hsg_ref)):
        v = hs_ref[prev + st0:prev + st0 + st_rows, :]
        if dup_cols:
            shifted = jnp.concatenate([v[:, dup_cols:], jnp.zeros((st_rows, dup_cols), F32)], axis=1)
            v = jnp.where(j == nj - 1, shifted, v)
        if st_slabs:
            nb = st_rows // st_slabs
            for s in range(st_slabs):
                st_ref[:, s, :] = v[s * nb:(s + 1) * nb, :]
        else:
            st_ref[0] = v

    def conv(hs_ref, cw_ref, cb_ref, r0):
        return (cw_ref[0, 2:3, :] * hs_ref[prev + r0:prev + r0 + rb, :]
                + cw_ref[0, 1:2, :] * hs_ref[prev - stride + r0:prev - stride + r0 + rb, :]
                + cw_ref[0, 0:1, :] * hs_ref[prev - 2 * stride + r0:prev - 2 * stride + r0 + rb, :]
                + cb_ref[0])

    col = jax.lax.broadcasted_iota(jnp.int32, (rb, fc), 1)
    keep = (col >= dup_cols) | (j < nj - 1)
    for h in range(n_split):
        for r0 in range(h * hm, (h + 1) * hm, rb):
            yu = conv(hsu_ref, cwu_ref, cbu_ref, r0)
            yg = conv(hsg_ref, cwg_ref, cbg_ref, r0)
            p_ref[r0:r0 + rb, :] = jnp.where(keep, yg * jax.nn.sigmoid(yg) * yu, 0.0).astype(BF16)
        rows = slice(h * hm, (h + 1) * hm)
        o_ref[rows, :] += jnp.dot(p_ref[rows, :], wdn_ref[0], preferred_element_type=F32)

    @pl.when(j == nj - 1)
    def _():
        x_rows = x_ref[prev:prev + tm, :] if halo else x_ref[...]
        o_ref[...] = _layer_norm(alpha * x_rows + o_ref[...], g_ref[...], beta_ref[...])


def _conv_ffn(x, hist, wup, cw, cb, wdn, g, beta, l, wl, *, tm, tiles_per_seg, stride, st0, st_rows, alpha, hist_base=0,
              st_slabs=0, side=(), seg_rows=None, single_buffer_rows=False):
    halo = seg_rows is not None
    d = x.shape[1]
    _, prev, _ = hist.shape
    d_ff = wdn.shape[1]
    fc = FF_CHUNK
    nblk = d_ff // LANES
    cblk = fc // LANES
    nj = pl.cdiv(d_ff, fc)
    dup_cols = nj * fc - d_ff
    n_tiles = x.shape[0] // seg_rows * tiles_per_seg if halo else x.shape[0] // tm
    rows = n_tiles * tm
    n_split = 2
    hm = tm // n_split
    rb = 32 if hm % 32 == 0 else 48
    assert d_ff % LANES == 0 and tm % n_split == 0 and hm % rb == 0 and nblk >= cblk
    row_mode = {"pipeline_mode": pl.Buffered(1)} if single_buffer_rows else {}

    def blk0(j):
        return jnp.minimum(j * cblk, nblk - cblk)

    el = pl.Element
    col_u = lambda j: LANES * blk0(j)
    col_g = lambda j: LANES * (nblk + blk0(j))
    casts = [_SideCast(sw, wl, r, n_tiles * nj, lambda i, j: i * nj + j) for sw, wl, r in side]
    kern = functools.partial(_ffn_kernel, tm=tm, prev=prev, stride=stride, tiles_per_seg=tiles_per_seg, st0=st0,
                             st_rows=st_rows, st_slabs=st_slabs, alpha=alpha, rb=rb, n_split=n_split,
                             dup_cols=dup_cols, n_side=len(casts), halo=halo)
    x_rows = tm + prev if halo else tm
    scratch = [pltpu.VMEM((x_rows, d), BF16), pltpu.VMEM((prev + tm, fc), F32), pltpu.VMEM((prev + tm, fc), F32),
               pltpu.VMEM((tm, fc), BF16)]
    if tiles_per_seg > 1 and not halo:
        scratch += [pltpu.VMEM((nj, prev, fc), F32), pltpu.VMEM((nj, prev, fc), F32)]
    if halo:
        assert prev % BF16_ROWS == 0 and seg_rows % BF16_ROWS == 0 and tm % BF16_ROWS == 0
        assert prev + tiles_per_seg * tm <= seg_rows
        x_spec = pl.BlockSpec((el(x_rows), el(d)), lambda i, j: (
            BF16_ROWS * ((i // tiles_per_seg) * (seg_rows // BF16_ROWS) + (i % tiles_per_seg) * (tm // BF16_ROWS)), 0),
            **row_mode)
    else:
        x_spec = pl.BlockSpec((tm, d), lambda i, j: (i, 0), **row_mode)
    hist_blk = (el(1), el(prev), el(fc))
    if st_slabs:
        assert n_tiles == 1 and st_rows % st_slabs == 0
        st_dims = (st_rows // st_slabs, st_slabs)
        st_spec = pl.BlockSpec(st_dims + (fc,), lambda i, j: (0, 0, j))
    else:
        st_dims = (n_tiles, st_rows)
        st_spec = pl.BlockSpec((1, st_rows, fc), lambda i, j: (i, 0, j))
    return pl.pallas_call(
        kern,
        grid=(n_tiles, nj),
        in_specs=[x_spec,
                  pl.BlockSpec(hist_blk, lambda i, j: (hist_base + i // tiles_per_seg, 0, col_u(j))),
                  pl.BlockSpec(hist_blk, lambda i, j: (hist_base + i // tiles_per_seg, 0, col_g(j))),
                  pl.BlockSpec((el(1), el(d), el(fc)), lambda i, j: (wl, 0, col_u(j))),
                  pl.BlockSpec((el(1), el(d), el(fc)), lambda i, j: (wl, 0, col_g(j))),
                  pl.BlockSpec((el(1), el(3), el(fc)), lambda i, j: (l, 0, col_u(j))),
                  pl.BlockSpec((el(1), el(3), el(fc)), lambda i, j: (l, 0, col_g(j))),
                  pl.BlockSpec((el(1), el(1), el(fc)), lambda i, j: (l, 0, col_u(j))),
                  pl.BlockSpec((el(1), el(1), el(fc)), lambda i, j: (l, 0, col_g(j))),
                  pl.BlockSpec((el(1), el(fc), el(d)), lambda i, j: (wl, col_u(j), 0)),
                  _layer_spec(l, 1, d), _layer_spec(l, 1, d)] + [c_.in_spec for c_ in casts],
        out_specs=[pl.BlockSpec((tm, d), lambda i, j: (i, 0), **row_mode), st_spec, st_spec]
        + [c_.out_spec for c_ in casts],
        out_shape=[jax.ShapeDtypeStruct((rows, d), F32),
                   jax.ShapeDtypeStruct(st_dims + (nj * fc,), F32),
                   jax.ShapeDtypeStruct(st_dims + (nj * fc,), F32)] + [c_.out_shape for c_ in casts],
        scratch_shapes=scratch,
        compiler_params=_params("arbitrary", "arbitrary"),
        name="conv_ffn",
    )(x, hist, hist, wup, wup, cw, cw, cb, cb, wdn, g, beta, *[c_.src for c_ in casts])


def kernel(x_prompt, x_sample, state_attn_k, state_attn_v, state_conv, state_ffn_conv, meta_tokens, w_in, b_in, attn_sinks, conv_w, conv_b, conv_ln_g, conv_ln_b, w_out, b_out, ln1_g, ln1_b, ffn_w_up, ffn_conv_w, ffn_conv_b, ffn_w_down, ln2_g, ln2_b):
    batch, seq, d_model = x_prompt.shape
    dec_batch, dec_seq, _ = x_sample.shape
    depth = w_in.shape[0]
    conv_ch = conv_w.shape[2]
    conv_taps = conv_w.shape[1]
    ffn_taps = ffn_conv_w.shape[1]
    attn_w = d_model - conv_ch
    kv_w = 2 * N_KV_HEADS * HEAD_DIM
    n_heads = attn_w // HEAD_DIM
    group = n_heads // N_KV_HEADS
    d_ff = ffn_w_down.shape[1]
    ct = conv_ch // LANES
    alpha = (2 * depth) ** 0.25
    assert ffn_taps == 3 and kv_w == 2 * LANES and dec_batch % SUBLANES == 0
    assert ffn_taps - 1 <= dec_seq <= min(conv_taps - 1, SUBLANES) and seq >= WINDOW >= conv_taps

    seq_all = N_META + seq
    lp = _round_up(seq_all, WINDOW)
    tiles_p = 6
    tm_p = lp // tiles_p
    assert tm_p % 64 == 0
    meta = jnp.broadcast_to(meta_tokens[None].astype(x_prompt.dtype), (batch, N_META, d_model))
    xp = jnp.concatenate([meta, x_prompt, jnp.zeros((batch, lp - seq_all, d_model), x_prompt.dtype)], axis=1)
    xp = xp.reshape(batch * lp, d_model)
    rows_s = dec_seq * dec_batch
    xs = jnp.swapaxes(x_sample, 0, 1).reshape(rows_s, d_model)

    conv_hist_rows = 32
    ffn_prev_p = SUBLANES
    last_p = seq_all - 1
    ffn_state_tile = last_p // tm_p
    ffn_st0 = (last_p % tm_p) // SUBLANES * SUBLANES
    assert (last_p - 1) // tm_p == ffn_state_tile and (last_p - 1) % tm_p >= ffn_st0

    rows3 = lambda v: v[:, None, :]
    b_in3, conv_b3, cg3, cbt3 = rows3(b_in), rows3(conv_b), rows3(conv_ln_g), rows3(conv_ln_b)
    b_out3, g1, be1, g2, be2 = rows3(b_out), rows3(ln1_g), rows3(ln1_b), rows3(ln2_g), rows3(ln2_b)
    fcb3 = rows3(ffn_conv_b)
    zero_conv_hist = jnp.zeros((batch, ct, conv_hist_rows, LANES), F32)
    zero_ffn_hist = jnp.zeros((batch, ffn_prev_p, 2 * d_ff), F32)
    tm_y = 1024
    tiles_y = seq // tm_y
    assert seq % tm_y == 0 and N_META % BF16_ROWS == 0
    zero_ffn_halo = jnp.zeros((batch, N_META, 2 * d_ff), F32)
    conv_hist_all = jnp.swapaxes(state_conv, 1, 2)
    st_rows = (ffn_taps - 1) * dec_batch
    ffn_hist_all = jnp.swapaxes(state_ffn_conv, 1, 2).reshape(depth, st_rows, 2 * d_ff)

    w_in_b = w_in[:1].astype(BF16)
    w_out_b = w_out[:1].astype(BF16)
    w_up_b = w_dn_b = None

    pk, pv, pc, pf, sf = [], [], [], [], []
    kv_state = conv_state = None
    for l in range(depth):
        sinks = attn_sinks[l]
        first = l == 0
        nxt = l + 1 < depth

        q, kv, u = _in_proj(xp, w_in_b, b_in3, l, 0, tm=tm_p, attn_w=attn_w, conv_ch=conv_ch, kv_w=kv_w)
        attn, = _attn_prompt(sinks, q, kv, n_seg=batch, blocks_per_seg=lp // WINDOW, blk=WINDOW)
        c, *cast = _conv_module(zero_conv_hist, u, conv_w, conv_b3, cg3, cbt3, l, tm=tm_p, tiles_per_seg=tiles_p,
                                side=[(ffn_w_up, 0, 176)] if first else [])
        if first:
            w_up_b = cast[0][None]
        x1, *cast = _out_proj(attn, c, xp, w_out_b, b_out3, g1, be1, l, 0, tm=tm_p, alpha=alpha,
                              side=[(ffn_w_down, 0, 512)] if first else [])
        if first:
            w_dn_b = cast[0][None]
        if nxt:
            next_w = [(w_in, l + 1, 16), (w_out, l + 1, 16), (ffn_w_up, l + 1, 16), (ffn_w_down, l + 1, 48)]
            xp, su, sg, *cast = _conv_ffn(x1, zero_ffn_hist, w_up_b, ffn_conv_w, fcb3, w_dn_b, g2, be2, l, 0, tm=tm_p,
                                          tiles_per_seg=tiles_p, stride=1, st0=ffn_st0, st_rows=SUBLANES,
                                          alpha=alpha, side=next_w)
            st_tile, st_off, st_tiles = ffn_state_tile, last_p % tm_p - ffn_st0 - (ffn_taps - 2), tiles_p
        else:
            y_rows, su, sg = _conv_ffn(x1, zero_ffn_halo, w_up_b, ffn_conv_w, fcb3, w_dn_b, g2, be2, l, 0, tm=tm_y,
                                       tiles_per_seg=tiles_y, stride=1, st0=tm_y - SUBLANES, st_rows=SUBLANES,
                                       alpha=alpha, seg_rows=lp, single_buffer_rows=True)
            st_tile, st_off, st_tiles = tiles_y - 1, SUBLANES - (ffn_taps - 1), tiles_y
        kv3 = kv.reshape(batch, lp, kv_w)[:, seq_all - WINDOW:seq_all]
        pk.append(kv3[..., :kv_w // 2].reshape(batch, WINDOW, N_KV_HEADS, HEAD_DIM))
        pv.append(kv3[..., kv_w // 2:].reshape(batch, WINDOW, N_KV_HEADS, HEAD_DIM))
        u_tail = u.reshape(ct, batch, lp, LANES)[:, :, seq_all - (conv_taps - 1):seq_all]
        pc.append(u_tail.transpose(1, 2, 0, 3).reshape(batch, conv_taps - 1, conv_ch))
        hst = jnp.concatenate([su[..., :d_ff], sg[..., :d_ff]], axis=-1).reshape(batch, st_tiles, SUBLANES, 2 * d_ff)
        pf.append(hst[:, st_tile, st_off:st_off + ffn_taps - 1])

        q, kv, u = _in_proj(xs, w_in_b, b_in3, l, 0, tm=rows_s, attn_w=attn_w, conv_ch=conv_ch, kv_w=kv_w)
        q5 = q.reshape(dec_seq, dec_batch, N_KV_HEADS, group, HEAD_DIM).transpose(1, 2, 0, 3, 4)
        q5 = q5.reshape(dec_batch, N_KV_HEADS, dec_seq * group, HEAD_DIM)
        kv_new = kv.reshape(dec_seq, dec_batch, kv_w).transpose(1, 0, 2)
        kv_new_pad = jnp.pad(kv_new, ((0, 0), (0, SUBLANES - dec_seq), (0, 0)))
        new_k = kv_new_pad[..., :kv_w // 2].reshape(dec_batch, SUBLANES, N_KV_HEADS, HEAD_DIM)
        new_v = kv_new_pad[..., kv_w // 2:].reshape(dec_batch, SUBLANES, N_KV_HEADS, HEAD_DIM)
        o, *kv_state = _attn_sample(sinks, q5, state_attn_k, state_attn_v, new_k, new_v, l, kv_state,
                                    t_steps=dec_seq, bb=4)
        o = o.reshape(dec_batch, N_KV_HEADS, dec_seq, group, HEAD_DIM)
        attn = o.transpose(2, 0, 1, 3, 4).reshape(rows_s, attn_w).astype(BF16)
        u4 = u.reshape(ct, dec_seq, dec_batch, LANES)
        c, conv_state = _conv_module_sample(conv_hist_all, u4, conv_w, conv_b3, cg3, cbt3, l, conv_state, bb=32)
        c = c.reshape(rows_s, conv_ch)
        x1, = _out_proj(attn, c, xs, w_out_b, b_out3, g1, be1, l, 0, tm=rows_s, alpha=alpha)
        xs, su, sg = _conv_ffn(x1, ffn_hist_all, w_up_b, ffn_conv_w, fcb3, w_dn_b, g2, be2, l, 0, tm=rows_s,
                               tiles_per_seg=1, stride=dec_batch, st0=rows_s - st_rows, st_rows=st_rows,
                               alpha=alpha, hist_base=l, st_slabs=ffn_taps - 1)
        sf.append(jnp.concatenate([su[..., :d_ff], sg[..., :d_ff]], axis=-1))
        if nxt:
            w_in_b, w_out_b, w_up_b, w_dn_b = (w[None] for w in cast)

    y_prompt = y_rows.reshape(batch, seq, d_model)
    y_sample = jnp.swapaxes(xs.reshape(dec_seq, dec_batch, d_model), 0, 1)
    return (y_prompt, y_sample, jnp.stack(pk), jnp.stack(pv), jnp.stack(pc), jnp.stack(pf),
            kv_state[0], kv_state[1], conv_state, jnp.stack(sf))
```
